```python
import math
import jax, jax.numpy as jnp
from jax import lax
import numpy as np

D_MODEL = 2048
BATCH = 8
SEQ = 2048
DEPTH = 1
DEC_BATCH = 32
DEC_SEQ = 8
PAST_LEN = 8192
PAGE_SIZE = 128

HEAD_DIM = 128
N_HEADS = D_MODEL // HEAD_DIM
H_MOBA = N_HEADS // 2
H_DIFF = N_HEADS - H_MOBA
MOBA_WIDTH = H_MOBA * HEAD_DIM
DIFF_WIDTH = H_DIFF * HEAD_DIM
DIFF_QK = HEAD_DIM // 2
DIFF_V = HEAD_DIM
MOBA_BLOCK = 256
MOBA_TOPK = 3
MOBA_Q_CHUNK = 64
Q_BLOCK = 128
D_FF = -(-8 * D_MODEL // (3 * 256)) * 256
RMS_EPS = 1e-6
IN_WIDTH = 3 * MOBA_WIDTH + 3 * DIFF_WIDTH

kernel_name = 'hymba_moba_diffattn_decode_step'

F32 = jnp.float32


def rms_norm(x, g):
    xf = x.astype(F32)
    y = xf * lax.rsqrt(jnp.mean(xf * xf, axis=-1, keepdims=True) + RMS_EPS)
    return (y * g.astype(F32)).astype(x.dtype)


def alibi_slopes():
    return 2.0 ** (-8.0 * jnp.arange(1, N_HEADS + 1, dtype=F32) / N_HEADS)


def lambda_init(layer):
    return 0.8 - 0.6 * math.exp(-0.3 * layer)


def project(h, w):
    b, t, _ = h.shape
    z = h @ w
    o1 = MOBA_WIDTH
    o2 = 2 * MOBA_WIDTH
    o3 = 3 * MOBA_WIDTH
    o4 = o3 + DIFF_WIDTH
    o5 = o4 + DIFF_WIDTH
    mq = z[..., :o1].reshape(b, t, H_MOBA, HEAD_DIM)
    mk = z[..., o1:o2].reshape(b, t, H_MOBA, HEAD_DIM)
    mv = z[..., o2:o3].reshape(b, t, H_MOBA, HEAD_DIM)
    dq = z[..., o3:o4].reshape(b, t, H_DIFF, 2, DIFF_QK)
    dk = z[..., o4:o5].reshape(b, t, H_DIFF, 2, DIFF_QK)
    dv = z[..., o5:].reshape(b, t, H_DIFF, DIFF_V)
    return mq, mk, mv, dq, dk, dv


def moba_attend(q, k, v, q_pos, slopes):
    tq = q.shape[0]
    n_blk = k.shape[0] // MOBA_BLOCK
    kb = k.reshape(n_blk, MOBA_BLOCK, H_MOBA, HEAD_DIM).transpose(2, 0, 1, 3)
    vb = v.reshape(n_blk, MOBA_BLOCK, H_MOBA, HEAD_DIM).transpose(2, 0, 1, 3)
    k_mean = jnp.mean(kb.astype(F32), axis=2)
    gate = jnp.einsum('qhd,hnd->qhn', q.astype(F32), k_mean)
    cur = q_pos // MOBA_BLOCK
    past = jnp.arange(n_blk)[None, None, :] < cur[:, None, None]
    gate = jnp.where(past, gate, -jnp.inf)
    n_top = min(MOBA_TOPK, n_blk)
    _, top_idx = lax.top_k(gate, n_top)
    top_ok = jnp.broadcast_to(jnp.arange(n_top)[None, None, :] < cur[:, None, None],
                              (tq, H_MOBA, n_top))
    sel = jnp.concatenate([top_idx.astype(jnp.int32),
                           jnp.broadcast_to(cur[:, None, None], (tq, H_MOBA, 1)).astype(jnp.int32)], axis=-1)
    ok = jnp.concatenate([top_ok, jnp.ones((tq, H_MOBA, 1), dtype=bool)], axis=-1)
    h_ix = jnp.arange(H_MOBA)[None, :, None]
    k_sel = kb[h_ix, sel]
    v_sel = vb[h_ix, sel]
    k_pos = sel[..., None] * MOBA_BLOCK + jnp.arange(MOBA_BLOCK, dtype=jnp.int32)
    dist = (q_pos[:, None, None, None] - k_pos).astype(F32)
    logits = (jnp.einsum('qhd,qhsjd->qhsj', q, k_sel).astype(F32) * (HEAD_DIM ** -0.5)
              - slopes[None, :, None, None] * dist)
    logits = jnp.where(ok[..., None] & (dist >= 0), logits, -jnp.inf)
    p = jax.nn.softmax(logits.reshape(tq, H_MOBA, -1), axis=-1).reshape(logits.shape)
    return jnp.einsum('qhsj,qhsjd->qhd', p.astype(v.dtype), v_sel)


def pad_to_block(a):
    pad = (-a.shape[0]) % MOBA_BLOCK
    return jnp.pad(a, ((0, pad), (0, 0), (0, 0)))


def moba_prompt(q, k, v, pos, slopes):
    n_chunk = SEQ // MOBA_Q_CHUNK
    pos_c = pos.reshape(n_chunk, MOBA_Q_CHUNK)

    def one_seq(a):
        q_b, k_b, v_b = a
        kp = pad_to_block(k_b)
        vp = pad_to_block(v_b)
        qc = q_b.reshape(n_chunk, MOBA_Q_CHUNK, H_MOBA, HEAD_DIM)
        o = lax.map(lambda c: moba_attend(c[0], kp, vp, c[1], slopes), (qc, pos_c))
        return o.reshape(SEQ, H_MOBA, HEAD_DIM)

    return lax.map(one_seq, (q, k, v))


def moba_sample(q, k, v, pool_k, pool_v, page_table, pos, slopes):
    def one_seq(a):
        q_b, k_b, v_b, pt = a
        k_all = pad_to_block(jnp.concatenate([pool_k[pt].reshape(-1, H_MOBA, HEAD_DIM), k_b], axis=0))
        v_all = pad_to_block(jnp.concatenate([pool_v[pt].reshape(-1, H_MOBA, HEAD_DIM), v_b], axis=0))
        return moba_attend(q_b, k_all, v_all, pos, slopes)

    return lax.map(one_seq, (q, k, v, page_table))


def diff_attend(q, k, v, q_pos, k_pos, slopes, lam):
    dist = (q_pos[:, None] - k_pos[None, :]).astype(F32)
    logits = (jnp.einsum('qhcd,khcd->chqk', q, k).astype(F32) * (DIFF_QK ** -0.5)
              - slopes[None, :, None, None] * dist)
    logits = jnp.where(dist >= 0, logits, -jnp.inf)
    p = jax.nn.softmax(logits, axis=-1)
    a = p[0] - lam * p[1]
    return jnp.einsum('hqk,khd->qhd', a.astype(v.dtype), v)


def diff_prompt(q, k, v, pos, slopes, lam):
    b = q.shape[0]
    n_qb = SEQ // Q_BLOCK
    qb = q.reshape(b, n_qb, Q_BLOCK, H_DIFF, 2, DIFF_QK).transpose(1, 0, 2, 3, 4, 5)
    pb = pos.reshape(n_qb, Q_BLOCK)
    attend = jax.vmap(diff_attend, in_axes=(0, 0, 0, None, None, None, None))
    o = lax.map(lambda a: attend(a[0], k, v, a[1], pos, slopes, lam), (qb, pb))
    return o.transpose(1, 0, 2, 3, 4).reshape(b, SEQ, H_DIFF, DIFF_V)


def diff_sample(q, k, v, pool_k, pool_v, page_table, pos, k_pos, slopes, lam):
    def one_seq(a):
        q_b, k_b, v_b, pt = a
        k_all = jnp.concatenate([pool_k[pt].reshape(-1, H_DIFF, 2, DIFF_QK), k_b], axis=0)
        v_all = jnp.concatenate([pool_v[pt].reshape(-1, H_DIFF, DIFF_V), v_b], axis=0)
        return diff_attend(q_b, k_all, v_all, pos, k_pos, slopes, lam)

    return lax.map(one_seq, (q, k, v, page_table))


def merge_heads(o_moba, o_diff, g_sub, lam_init, w_o, dtype):
    b, t = o_moba.shape[:2]
    o_diff = rms_norm(o_diff, g_sub) * (1.0 - lam_init)
    o = jnp.concatenate([o_moba.reshape(b, t, MOBA_WIDTH).astype(dtype),
                         o_diff.reshape(b, t, DIFF_WIDTH).astype(dtype)], axis=-1)
    return o @ w_o


def swiglu(h, wg, wu, wd):
    return (jax.nn.silu(h @ wg) * (h @ wu)) @ wd


def setup_inputs(seed: int = 0) -> dict:
    key = jax.random.key(seed)
    ks = jax.random.split(key, 24)
    n_pages = PAST_LEN // PAGE_SIZE
    n_used = DEC_BATCH * n_pages
    n_pool = n_used + n_used // 4 + 1

    def nrm(k, shape, scale=1.0):
        return jax.random.normal(k, shape, dtype=F32) * scale

    def gain(k, n):
        return 1.0 + nrm(k, (DEPTH, n), 0.02)

    page_table = jax.random.permutation(ks[6], n_pool)[:n_used].reshape(DEC_BATCH, n_pages).astype(jnp.int32)
    return {
        'x_prompt': nrm(ks[0], (BATCH, SEQ, D_MODEL)),
        'x_sample': nrm(ks[1], (DEC_BATCH, DEC_SEQ, D_MODEL)),
        'cache_moba_k': nrm(ks[2], (DEPTH, n_pool, PAGE_SIZE, H_MOBA, HEAD_DIM)),
        'cache_moba_v': nrm(ks[3], (DEPTH, n_pool, PAGE_SIZE, H_MOBA, HEAD_DIM)),
        'cache_diff_k': nrm(ks[4], (DEPTH, n_pool, PAGE_SIZE, H_DIFF, 2 * DIFF_QK)),
        'cache_diff_v': nrm(ks[5], (DEPTH, n_pool, PAGE_SIZE, H_DIFF, DIFF_V)),
        'page_table': page_table,
        'g_pre_attn': gain(ks[7], D_MODEL),
        'w_in': nrm(ks[8], (DEPTH, D_MODEL, IN_WIDTH), D_MODEL ** -0.5),
        'lambda_q1': nrm(ks[9], (DEPTH, DIFF_QK), 0.1),
        'lambda_k1': nrm(ks[10], (DEPTH, DIFF_QK), 0.1),
        'lambda_q2': nrm(ks[11], (DEPTH, DIFF_QK), 0.1),
        'lambda_k2': nrm(ks[12], (DEPTH, DIFF_QK), 0.1),
        'g_subln': gain(ks[13], DIFF_V),
        'w_o': nrm(ks[14], (DEPTH, D_MODEL, D_MODEL), D_MODEL ** -0.5),
        'g_post_attn': gain(ks[15], D_MODEL),
        'g_pre_ffn': gain(ks[16], D_MODEL),
        'w_gate': nrm(ks[17], (DEPTH, D_MODEL, D_FF), D_MODEL ** -0.5),
        'w_up': nrm(ks[18], (DEPTH, D_MODEL, D_FF), D_MODEL ** -0.5),
        'w_down': nrm(ks[19], (DEPTH, D_FF, D_MODEL), D_FF ** -0.5),
        'g_post_ffn': gain(ks[20], D_MODEL),
    }


def reference(x_prompt, x_sample, cache_moba_k, cache_moba_v, cache_diff_k, cache_diff_v, page_table,
              g_pre_attn, w_in, lambda_q1, lambda_k1, lambda_q2, lambda_k2, g_subln, w_o, g_post_attn,
              g_pre_ffn, w_gate, w_up, w_down, g_post_ffn):
    slopes = alibi_slopes()
    s_moba = slopes[0::2]
    s_diff = slopes[1::2]
    pos_p = jnp.arange(SEQ, dtype=jnp.int32)
    pos_s = PAST_LEN + jnp.arange(DEC_SEQ, dtype=jnp.int32)
    kpos_s = jnp.arange(PAST_LEN + DEC_SEQ, dtype=jnp.int32)
    xp = x_prompt
    xs = x_sample
    mk_p, mv_p, dk_p, dv_p = [], [], [], []
    mk_s, mv_s, dk_s, dv_s = [], [], [], []
    for l in range(DEPTH):
        lam_init = lambda_init(l)
        lam = (jnp.exp(jnp.sum(lambda_q1[l].astype(F32) * lambda_k1[l].astype(F32)))
               - jnp.exp(jnp.sum(lambda_q2[l].astype(F32) * lambda_k2[l].astype(F32))) + lam_init)

        mq, mk, mv, dq, dk, dv = project(rms_norm(xp, g_pre_attn[l]), w_in[l])
        o_m = moba_prompt(mq, mk, mv, pos_p, s_moba)
        o_d = diff_prompt(dq, dk, dv, pos_p, s_diff, lam)
        xp = xp + rms_norm(merge_heads(o_m, o_d, g_subln[l], lam_init, w_o[l], xp.dtype), g_post_attn[l])
        xp = xp + rms_norm(swiglu(rms_norm(xp, g_pre_ffn[l]), w_gate[l], w_up[l], w_down[l]), g_post_ffn[l])
        mk_p.append(mk)
        mv_p.append(mv)
        dk_p.append(dk.reshape(BATCH, SEQ, H_DIFF, 2 * DIFF_QK))
        dv_p.append(dv)

        mq, mk, mv, dq, dk, dv = project(rms_norm(xs, g_pre_attn[l]), w_in[l])
        o_m = moba_sample(mq, mk, mv, cache_moba_k[l], cache_moba_v[l], page_table, pos_s, s_moba)
        o_d = diff_sample(dq, dk, dv, cache_diff_k[l], cache_diff_v[l], page_table, pos_s, kpos_s, s_diff, lam)
        xs = xs + rms_norm(merge_heads(o_m, o_d, g_subln[l], lam_init, w_o[l], xs.dtype), g_post_attn[l])
        xs = xs + rms_norm(swiglu(rms_norm(xs, g_pre_ffn[l]), w_gate[l], w_up[l], w_down[l]), g_post_ffn[l])
        mk_s.append(mk)
        mv_s.append(mv)
        dk_s.append(dk.reshape(DEC_BATCH, DEC_SEQ, H_DIFF, 2 * DIFF_QK))
        dv_s.append(dv)

    return (xp, xs,
            jnp.stack(mk_p), jnp.stack(mv_p), jnp.stack(dk_p), jnp.stack(dv_p),
            jnp.stack(mk_s), jnp.stack(mv_s), jnp.stack(dk_s), jnp.stack(dv_s))
```

```python
import functools
import math

import jax
import jax.numpy as jnp
from jax import lax
from jax.experimental import pallas as pl
from jax.experimental.pallas import tpu as pltpu

F32 = jnp.float32
BF16 = jnp.bfloat16
NEG_INF = float("-inf")

HEAD_DIM = 128
MOBA_BLOCK = 256
MOBA_TOPK = 3
RMS_EPS = 1e-6
PAGES_PER_STEP = 4
VMEM_LIMIT = 56 * 1024 * 1024


def _alibi_slopes(n_heads):
    return 2.0 ** (-8.0 * jnp.arange(1, n_heads + 1, dtype=F32) / n_heads)


def _lambda_init(layer):
    return 0.8 - 0.6 * math.exp(-0.3 * layer)


def _rms(x, g):
    return (x * lax.rsqrt(jnp.mean(x * x, axis=-1, keepdims=True) + RMS_EPS)) * g


def _dot(a, b):
    return jnp.dot(a, b, preferred_element_type=F32)


def _proj_kernel(x_ref, g_ref, w_ref, *refs):
    outs, h_ref = refs[:-1], refs[-1]
    j = pl.program_id(1)

    @pl.when(j == 0)
    def _():
        h_ref[...] = _rms(x_ref[...], g_ref[...]).astype(BF16)

    for k, o_ref in enumerate(outs):
        @pl.when(j == k)
        def _(o_ref=o_ref):
            o_ref[...] = _dot(h_ref[...], w_ref[...])


def _proj(x, g, w, tm):
    m, d = x.shape
    n_out = w.shape[1] // 1024
    return pl.pallas_call(
        _proj_kernel,
        grid=(m // tm, n_out),
        in_specs=[
            pl.BlockSpec((tm, d), lambda i, j: (i, 0)),
            pl.BlockSpec((1, d), lambda i, j: (0, 0)),
            pl.BlockSpec((d, 1024), lambda i, j: (0, j)),
        ],
        out_specs=[pl.BlockSpec((tm, 1024), lambda i, j: (i, 0)) for _ in range(n_out)],
        out_shape=[jax.ShapeDtypeStruct((m, 1024), F32) for _ in range(n_out)],
        scratch_shapes=[pltpu.VMEM((tm, d), BF16)],
        compiler_params=pltpu.CompilerParams(
            dimension_semantics=("arbitrary", "arbitrary"), vmem_limit_bytes=VMEM_LIMIT),
        name="proj",
    )(x, g, w)


def _local_bias(slope):
    t = lax.broadcasted_iota(jnp.int32, (MOBA_BLOCK, MOBA_BLOCK), 1)
    j = lax.broadcasted_iota(jnp.int32, (MOBA_BLOCK, MOBA_BLOCK), 0)
    d = (t - j).astype(F32)
    b0 = -slope * d
    return b0, jnp.where(d >= 0, b0, NEG_INF)


def _moba_prompt_kernel(slopes_ref, q_ref, k_ref, v_ref, o_ref, kb_ref, vt_ref, selb_ref):
    nb = kb_ref.shape[0]
    slope = slopes_ref[pl.program_id(1)]
    scale = HEAD_DIM ** -0.5
    means = []
    for n in range(nb):
        kblk = k_ref[n * MOBA_BLOCK:(n + 1) * MOBA_BLOCK, :]
        kb_ref[n] = kblk.astype(BF16)
        vt_ref[n] = jnp.transpose(v_ref[n * MOBA_BLOCK:(n + 1) * MOBA_BLOCK, :]).astype(BF16)
        means.append(jnp.mean(kblk, axis=0, keepdims=True))
    km = jnp.concatenate(means, axis=0)
    km_hi = km.astype(BF16)
    km_lo = (km - km_hi.astype(F32)).astype(BF16)
    b0, b0_diag = _local_bias(slope)
    blk = lax.broadcasted_iota(jnp.int32, (nb, MOBA_BLOCK), 0)

    def q_block(i, carry):
        qt = jnp.transpose(q_ref[pl.ds(i * MOBA_BLOCK, MOBA_BLOCK), :]).astype(BF16)
        gate = _dot(km_hi, qt) + _dot(km_lo, qt)
        cnt = jnp.zeros(gate.shape, F32)
        for m in range(nb):
            gm = gate[m:m + 1, :]
            beats = (gm > gate) | ((gm == gate) & (m < blk))
            cnt = cnt + jnp.where(beats, jnp.where(m < i, 1.0, 0.0), 0.0)
        sel = (blk < i) & (cnt < MOBA_TOPK)
        selb_ref[...] = jnp.where(sel, 0.0, NEG_INF)

        s = _dot(kb_ref[i], qt) * scale + b0_diag
        m_run = jnp.max(s, axis=0, keepdims=True)
        p = jnp.exp(s - m_run)
        l_run = jnp.sum(p, axis=0, keepdims=True)
        acc = _dot(vt_ref[i], p.astype(BF16))

        def kv_block(n, c):
            m_run, l_run, acc = c
            off = selb_ref[pl.ds(n, 1), :] - slope * ((i - n) * MOBA_BLOCK).astype(F32)
            s = _dot(kb_ref[n], qt) * scale + b0
            m_new = jnp.maximum(m_run, jnp.max(s, axis=0, keepdims=True) + off)
            alpha = jnp.exp(m_run - m_new)
            p = jnp.exp(s - (m_new - off))
            l_new = l_run * alpha + jnp.sum(p, axis=0, keepdims=True)
            acc = acc * alpha + _dot(vt_ref[n], p.astype(BF16))
            return m_new, l_new, acc

        m_run, l_run, acc = lax.fori_loop(0, i, kv_block, (m_run, l_run, acc))
        o = jnp.transpose(acc / l_run)
        o_ref[pl.ds(i * MOBA_BLOCK, MOBA_BLOCK), :] = o.astype(o_ref.dtype)
        return carry

    lax.fori_loop(0, nb, q_block, 0)


def _moba_prompt(q, k, v, slopes, n_heads):
    b, t, _ = q.shape
    nb = t // MOBA_BLOCK
    spec = pl.BlockSpec((None, t, HEAD_DIM), lambda bi, h: (bi, 0, h))
    return pl.pallas_call(
        _moba_prompt_kernel,
        grid=(b, n_heads),
        in_specs=[pl.BlockSpec(memory_space=pltpu.SMEM), spec, spec, spec],
        out_specs=spec,
        out_shape=jax.ShapeDtypeStruct((b, t, n_heads * HEAD_DIM), BF16),
        scratch_shapes=[
            pltpu.VMEM((nb, MOBA_BLOCK, HEAD_DIM), BF16),
            pltpu.VMEM((nb, HEAD_DIM, MOBA_BLOCK), BF16),
            pltpu.VMEM((nb, MOBA_BLOCK), F32),
        ],
        compiler_params=pltpu.CompilerParams(
            dimension_semantics=("arbitrary", "arbitrary"), vmem_limit_bytes=VMEM_LIMIT),
        name="moba_prompt",
    )(slopes, q, k, v)


def _lambda(lq1_ref, lk1_ref, lq2_ref, lk2_ref, lam_init):
    a = jnp.sum(lq1_ref[...] * lk1_ref[...], axis=-1, keepdims=True)
    b = jnp.sum(lq2_ref[...] * lk2_ref[...], axis=-1, keepdims=True)
    return jnp.exp(a) - jnp.exp(b) + lam_init


def _diff_prompt_kernel(slopes_ref, lq1_ref, lk1_ref, lq2_ref, lk2_ref, gsub_ref, q_ref, k_ref, v_ref,
                        o_ref, kb_ref, vt_ref, *, lam_init):
    nb = kb_ref.shape[0]
    qk_dim = HEAD_DIM // 2
    slope = slopes_ref[pl.program_id(1)]
    lam = _lambda(lq1_ref, lk1_ref, lq2_ref, lk2_ref, lam_init)
    for n in range(nb):
        kb_ref[n] = k_ref[n * MOBA_BLOCK:(n + 1) * MOBA_BLOCK, :].astype(BF16)
        vt_ref[n] = jnp.transpose(v_ref[n * MOBA_BLOCK:(n + 1) * MOBA_BLOCK, :]).astype(BF16)
    b0, b0_diag = _local_bias(slope)
    b0 = jnp.concatenate([b0, b0], axis=1)
    b0_diag = jnp.concatenate([b0_diag, b0_diag], axis=1)
    row = lax.broadcasted_iota(jnp.int32, (HEAD_DIM, MOBA_BLOCK), 0)

    def q_block(i, carry):
        qt = jnp.transpose(q_ref[pl.ds(i * MOBA_BLOCK, MOBA_BLOCK), :]) * (qk_dim ** -0.5)
        qt2 = jnp.concatenate([jnp.where(row < qk_dim, qt, 0.0), jnp.where(row >= qk_dim, qt, 0.0)],
                              axis=1).astype(BF16)
        s = _dot(kb_ref[i], qt2) + b0_diag
        m_run = jnp.max(s, axis=0, keepdims=True)
        p = jnp.exp(s - m_run)
        l_run = jnp.sum(p, axis=0, keepdims=True)
        acc = _dot(vt_ref[i], p.astype(BF16))

        def kv_block(n, c):
            m_run, l_run, acc = c
            off = -slope * ((i - n) * MOBA_BLOCK).astype(F32)
            s = _dot(kb_ref[n], qt2) + b0
            m_new = jnp.maximum(m_run, jnp.max(s, axis=0, keepdims=True) + off)
            alpha = jnp.exp(m_run - m_new)
            p = jnp.exp(s - (m_new - off))
            l_new = l_run * alpha + jnp.sum(p, axis=0, keepdims=True)
            acc = acc * alpha + _dot(vt_ref[n], p.astype(BF16))
            return m_new, l_new, acc

        m_run, l_run, acc = lax.fori_loop(0, i, kv_block, (m_run, l_run, acc))
        on = acc / l_run
        o = jnp.transpose(on[:, :MOBA_BLOCK] - lam * on[:, MOBA_BLOCK:])
        o = _rms(o, gsub_ref[...]) * (1.0 - lam_init)
        o_ref[pl.ds(i * MOBA_BLOCK, MOBA_BLOCK), :] = o.astype(o_ref.dtype)
        return carry

    lax.fori_loop(0, nb, q_block, 0)


def _diff_prompt(q, k, v, slopes, lam_params, gsub, lam_init, n_heads):
    b, t, _ = q.shape
    nb = t // MOBA_BLOCK
    spec = pl.BlockSpec((None, t, HEAD_DIM), lambda bi, h: (bi, 0, h))
    small = lambda a: pl.BlockSpec(a.shape, lambda bi, h: (0, 0))
    return pl.pallas_call(
        functools.partial(_diff_prompt_kernel, lam_init=lam_init),
        grid=(b, n_heads),
        in_specs=[pl.BlockSpec(memory_space=pltpu.SMEM)] + [small(a) for a in lam_params] + [small(gsub)]
                 + [spec, spec, spec],
        out_specs=spec,
        out_shape=jax.ShapeDtypeStruct((b, t, n_heads * HEAD_DIM), BF16),
        scratch_shapes=[
            pltpu.VMEM((nb, MOBA_BLOCK, HEAD_DIM), BF16),
            pltpu.VMEM((nb, HEAD_DIM, MOBA_BLOCK), BF16),
        ],
        compiler_params=pltpu.CompilerParams(
            dimension_semantics=("arbitrary", "arbitrary"), vmem_limit_bytes=VMEM_LIMIT),
        name="diff_prompt",
    )(slopes, *lam_params, gsub, q, k, v)


def _sample_common(j, n_ksteps, qbd_ref, knew_ref, k_refs, s_ref):
    page = k_refs[0].shape[0]

    @pl.when(j < n_ksteps)
    def _():
        for g, k_ref in enumerate(k_refs):
            row0 = pl.multiple_of((j * len(k_refs) + g) * page, page)
            s_ref[pl.ds(row0, page), :] = _dot(k_ref[...].astype(BF16), qbd_ref[...])

    @pl.when(j == n_ksteps - 1)
    def _():
        past = n_ksteps * len(k_refs) * page
        knew = knew_ref[...]
        pad = jnp.zeros((page - knew.shape[0], knew.shape[1]), knew.dtype)
        s_ref[past:past + page, :] = _dot(jnp.concatenate([knew, pad], axis=0).astype(BF16), qbd_ref[...])


def _softmax_in_place(s_ref, rl_ref, n_chunks, chunk, logits_of):
    lanes = s_ref.shape[1]

    def pass1(c, m):
        r0 = pl.multiple_of(c * chunk, chunk)
        s = logits_of(c, s_ref[pl.ds(r0, chunk), :])
        s_ref[pl.ds(r0, chunk), :] = s
        return jnp.maximum(m, jnp.max(s, axis=0, keepdims=True))

    m = lax.fori_loop(0, n_chunks, pass1, jnp.full((1, lanes), NEG_INF, F32))

    def pass2(c, l):
        r0 = pl.multiple_of(c * chunk, chunk)
        p = jnp.exp(s_ref[pl.ds(r0, chunk), :] - m)
        s_ref[pl.ds(r0, chunk), :] = p
        return l + jnp.sum(p, axis=0, keepdims=True)

    l = lax.fori_loop(0, n_chunks, pass2, jnp.zeros((1, lanes), F32))
    rl_ref[...] = jnp.broadcast_to(1.0 / l, rl_ref.shape)


def _pv_sweep(j, n_ksteps, vnew_ref, v_refs, s_ref, rl_ref, acc_ref):
    page = v_refs[0].shape[0]
    rows = page * len(v_refs)

    @pl.when(j == n_ksteps)
    def _():
        acc_ref[...] = jnp.zeros(acc_ref.shape, F32)

    @pl.when(j >= n_ksteps)
    def _():
        r0 = pl.multiple_of((j - n_ksteps) * rows, rows)
        pt = jnp.transpose(s_ref[pl.ds(r0, rows), :] * rl_ref[0:1, :]).astype(BF16)
        vb = jnp.concatenate([v_ref[...].astype(BF16) for v_ref in v_refs], axis=0)
        acc_ref[...] += _dot(pt, vb)

    @pl.when(j == 2 * n_ksteps - 1)
    def _():
        past = n_ksteps * rows
        pt = jnp.transpose(s_ref[past:past + page, :] * rl_ref[0:1, :]).astype(BF16)
        vnew = vnew_ref[...]
        pad = jnp.zeros((page - vnew.shape[0], vnew.shape[1]), vnew.dtype)
        acc_ref[...] += _dot(pt, jnp.concatenate([vnew, pad], axis=0).astype(BF16))


def _moba_sample_kernel(pt_ref, slope_ref, qbd_ref, knew_ref, vnew_ref, *refs, past_len, n_heads):
    del pt_ref
    g = PAGES_PER_STEP
    k_refs, v_refs = refs[:g], refs[g:2 * g]
    o_ref, s_ref, acc_ref, rl_ref, selb_ref = refs[2 * g:]
    page = k_refs[0].shape[0]
    dec = knew_ref.shape[0]
    n_ksteps = past_len // (page * g)
    n_past_blk = past_len // MOBA_BLOCK
    j = pl.program_id(1)
    _sample_common(j, n_ksteps, qbd_ref, knew_ref, k_refs, s_ref)

    @pl.when(j == n_ksteps - 1)
    def _():
        lanes = s_ref.shape[1]

        def gate_row(n, carry):
            r0 = pl.multiple_of(n * MOBA_BLOCK, MOBA_BLOCK)
            selb_ref[pl.ds(n, 1), :] = jnp.sum(s_ref[pl.ds(r0, MOBA_BLOCK), :], axis=0, keepdims=True) \
                * (1.0 / MOBA_BLOCK)
            return carry

        lax.fori_loop(0, n_past_blk, gate_row, 0)
        gate = selb_ref[0:n_past_blk, :]
        blk = lax.broadcasted_iota(jnp.int32, gate.shape, 0)
        cnt = jnp.zeros(gate.shape, F32)
        for m in range(n_past_blk):
            gm = gate[m:m + 1, :]
            beats = (gm > gate) | ((gm == gate) & (m < blk))
            cnt = cnt + jnp.where(beats, 1.0, 0.0)
        selb_ref[0:n_past_blk, :] = jnp.where(cnt < MOBA_TOPK, 0.0, NEG_INF)
        selb_ref[n_past_blk:n_past_blk + 8, :] = jnp.zeros((8, lanes), F32)

        lane = lax.broadcasted_iota(jnp.int32, (1, lanes), 1)
        q_pos = past_len + lane % dec
        slope = slope_ref[...]
        scale = HEAD_DIM ** -0.5

        def logits_of(c, raw):
            k_pos = c * page + lax.broadcasted_iota(jnp.int32, (page, 1), 0)
            dist = q_pos - k_pos
            sb = selb_ref[pl.ds(c * page // MOBA_BLOCK, 1), :]
            s = raw * scale - slope * dist.astype(F32) + sb
            return jnp.where(dist >= 0, s, NEG_INF)

        _softmax_in_place(s_ref, rl_ref, s_ref.shape[0] // page, page, logits_of)

    _pv_sweep(j, n_ksteps, vnew_ref, v_refs, s_ref, rl_ref, acc_ref)

    @pl.when(j == 2 * n_ksteps - 1)
    def _():
        acc = acc_ref[...]
        for h in range(n_heads):
            o_ref[:, h * HEAD_DIM:(h + 1) * HEAD_DIM] = \
                acc[h * dec:(h + 1) * dec, h * HEAD_DIM:(h + 1) * HEAD_DIM].astype(o_ref.dtype)


def _diff_sample_kernel(pt_ref, slope_ref, lq1_ref, lk1_ref, lq2_ref, lk2_ref, gsub_ref, qbd_ref, knew_ref,
                        vnew_ref, *refs, past_len, n_heads, lam_init):
    del pt_ref
    g = PAGES_PER_STEP
    k_refs, v_refs = refs[:g], refs[g:2 * g]
    o_ref, s_ref, acc_ref, rl_ref = refs[2 * g:]
    page = k_refs[0].shape[0]
    dec = knew_ref.shape[0]
    n_ksteps = past_len // (page * g)
    j = pl.program_id(1)
    _sample_common(j, n_ksteps, qbd_ref, knew_ref, k_refs, s_ref)

    @pl.when(j == n_ksteps - 1)
    def _():
        lanes = s_ref.shape[1]
        lane = lax.broadcasted_iota(jnp.int32, (1, lanes), 1)
        q_pos = past_len + lane % dec
        slope = slope_ref[...]

        def logits_of(c, raw):
            k_pos = c * page + lax.broadcasted_iota(jnp.int32, (page, 1), 0)
            dist = q_pos - k_pos
            return jnp.where(dist >= 0, raw - slope * dist.astype(F32), NEG_INF)

        _softmax_in_place(s_ref, rl_ref, s_ref.shape[0] // page, page, logits_of)

    _pv_sweep(j, n_ksteps, vnew_ref, v_refs, s_ref, rl_ref, acc_ref)

    @pl.when(j == 2 * n_ksteps - 1)
    def _():
        acc = acc_ref[...]
        lam = _lambda(lq1_ref, lk1_ref, lq2_ref, lk2_ref, lam_init)
        for h in range(n_heads):
            cols = slice(h * HEAD_DIM, (h + 1) * HEAD_DIM)
            r = 2 * h * dec
            o = acc[r:r + dec, cols] - lam * acc[r + dec:r + 2 * dec, cols]
            o_ref[:, cols] = (_rms(o, gsub_ref[...]) * (1.0 - lam_init)).astype(o_ref.dtype)


def _sample_attention(kern, extra_inputs, qbd, knew, vnew, pool_k, pool_v, page_table, layer, n_scratch_sel):
    nb, n_pages = page_table.shape
    page, width = pool_k.shape[2], pool_k.shape[3]
    dec = knew.shape[1]
    g = PAGES_PER_STEP
    n_ksteps = n_pages // g
    lanes = qbd.shape[2]

    def k_spec(gi):
        return pl.BlockSpec((None, None, page, width),
                            lambda b, j, pt: (layer, pt[b, jnp.minimum(j, n_ksteps - 1) * g + gi], 0, 0))

    def v_spec(gi):
        return pl.BlockSpec((None, None, page, width),
                            lambda b, j, pt: (layer, pt[b, jnp.maximum(j - n_ksteps, 0) * g + gi], 0, 0))

    small = lambda a: pl.BlockSpec(a.shape, lambda b, j, pt: (0,) * a.ndim)
    per_seq = lambda a: pl.BlockSpec((None,) + a.shape[1:], lambda b, j, pt: (b,) + (0,) * (a.ndim - 1))
    scratch = [
        pltpu.VMEM((n_pages * page + page, lanes), F32),
        pltpu.VMEM((lanes, width), F32),
        pltpu.VMEM((8, lanes), F32),
    ]
    if n_scratch_sel:
        scratch.append(pltpu.VMEM((n_scratch_sel, lanes), F32))
    return pl.pallas_call(
        kern,
        grid_spec=pltpu.PrefetchScalarGridSpec(
            num_scalar_prefetch=1,
            grid=(nb, 2 * n_ksteps),
            in_specs=[small(a) for a in extra_inputs] + [per_seq(qbd), per_seq(knew), per_seq(vnew)]
                     + [k_spec(gi) for gi in range(g)] + [v_spec(gi) for gi in range(g)],
            out_specs=pl.BlockSpec((None, dec, width), lambda b, j, pt: (b, 0, 0)),
            scratch_shapes=scratch,
        ),
        out_shape=jax.ShapeDtypeStruct((nb, dec, width), F32),
        compiler_params=pltpu.CompilerParams(
            dimension_semantics=("arbitrary", "arbitrary"), vmem_limit_bytes=VMEM_LIMIT),
        name=kern.func.__name__.strip("_"),
    )(page_table, *extra_inputs, qbd, knew, vnew, *([pool_k] * g), *([pool_v] * g))


def _block_diag_q(q, groups):
    b, dec, width = q.shape
    gd = width // groups
    t = q.reshape(b, dec, groups, gd).transpose(0, 2, 3, 1)
    eye = jnp.eye(groups, dtype=q.dtype)
    out = t[:, :, :, None, :] * eye[None, :, None, :, None]
    out = out.reshape(b, width, groups * dec)
    return jnp.pad(out, ((0, 0), (0, 0), (0, 128 - groups * dec)))


def _oproj_kernel(om_ref, od_ref, x_ref, w_ref, gpost_ref, gpre_ref, x1_ref, h2_ref):
    o = jnp.concatenate([om_ref[...].astype(BF16), od_ref[...].astype(BF16)], axis=1)
    x1 = x_ref[...] + _rms(_dot(o, w_ref[...]), gpost_ref[...])
    x1_ref[...] = x1
    h2_ref[...] = _rms(x1, gpre_ref[...]).astype(BF16)


def _oproj(om, od, x, w, gpost, gpre, tm):
    m, d = x.shape
    row = lambda width: pl.BlockSpec((tm, width), lambda i: (i, 0))
    const = lambda a: pl.BlockSpec(a.shape, lambda i: (0, 0))
    return pl.pallas_call(
        _oproj_kernel,
        grid=(m // tm,),
        in_specs=[row(om.shape[1]), row(od.shape[1]), row(d), const(w), const(gpost), const(gpre)],
        out_specs=[row(d), row(d)],
        out_shape=[jax.ShapeDtypeStruct((m, d), F32), jax.ShapeDtypeStruct((m, d), BF16)],
        compiler_params=pltpu.CompilerParams(
            dimension_semantics=("arbitrary",), vmem_limit_bytes=VMEM_LIMIT),
        name="oproj",
    )(om, od, x, w, gpost, gpre)


def _ffn_kernel(h_ref, x_ref, wg_ref, wu_ref, wd_ref, g_ref, o_ref, acc_ref):
    j = pl.program_id(1)
    h = h_ref[...]
    gate = _dot(h, wg_ref[...])
    up = _dot(h, wu_ref[...])
    a = (gate * (1.0 / (1.0 + jnp.exp(-gate))) * up).astype(BF16)
    part = _dot(a, wd_ref[...])

    @pl.when(j == 0)
    def _():
        acc_ref[...] = part

    @pl.when(j > 0)
    def _():
        acc_ref[...] += part

    @pl.when(j == pl.num_programs(1) - 1)
    def _():
        o_ref[...] = x_ref[...] + _rms(acc_ref[...], g_ref[...])


def _ffn(h, x, wg, wu, wd, g, tm, tf):
    m, d = x.shape
    f = wg.shape[1]
    return pl.pallas_call(
        _ffn_kernel,
        grid=(m // tm, f // tf),
        in_specs=[
            pl.BlockSpec((tm, d), lambda i, j: (i, 0)),
            pl.BlockSpec((tm, d), lambda i, j: (i, 0)),
            pl.BlockSpec((d, tf), lambda i, j: (0, j)),
            pl.BlockSpec((d, tf), lambda i, j: (0, j)),
            pl.BlockSpec((tf, d), lambda i, j: (j, 0)),
            pl.BlockSpec((1, d), lambda i, j: (0, 0)),
        ],
        out_specs=pl.BlockSpec((tm, d), lambda i, j: (i, 0)),
        out_shape=jax.ShapeDtypeStruct((m, d), F32),
        scratch_shapes=[pltpu.VMEM((tm, d), F32)],
        compiler_params=pltpu.CompilerParams(
            dimension_semantics=("arbitrary", "arbitrary"), vmem_limit_bytes=VMEM_LIMIT),
        name="ffn",
    )(h, x, wg, wu, wd, g)


def kernel(x_prompt, x_sample, cache_moba_k, cache_moba_v, cache_diff_k, cache_diff_v, page_table, g_pre_attn, w_in, lambda_q1, lambda_k1, lambda_q2, lambda_k2, g_subln, w_o, g_post_attn, g_pre_ffn, w_gate, w_up, w_down, g_post_ffn):
    batch, seq, d_model = x_prompt.shape
    dec_batch, dec_seq, _ = x_sample.shape
    depth, n_pool, page, h_moba, _ = cache_moba_k.shape
    h_diff = cache_diff_k.shape[3]
    n_heads = h_moba + h_diff
    wm, wd_ = h_moba * HEAD_DIM, h_diff * HEAD_DIM
    past_len = page_table.shape[1] * page
    assert seq % MOBA_BLOCK == 0 and past_len % MOBA_BLOCK == 0 and dec_seq <= MOBA_BLOCK
    assert wm == 1024 and wd_ == 1024 and h_moba * dec_seq <= 128 and 2 * h_diff * dec_seq <= 128

    slopes = _alibi_slopes(n_heads)
    s_moba, s_diff = slopes[0::2], slopes[1::2]
    pad128 = lambda v: jnp.pad(v, (0, 128 - v.shape[0])).reshape(1, 128)
    lane_slope_m = pad128(jnp.repeat(s_moba, dec_seq))
    lane_slope_d = pad128(jnp.repeat(s_diff, 2 * dec_seq))
    pools = [c.reshape(depth, n_pool, page, -1) for c in (cache_moba_k, cache_moba_v, cache_diff_k, cache_diff_v)]

    xp = x_prompt.reshape(batch * seq, d_model)
    xs = x_sample.reshape(dec_batch * dec_seq, d_model)
    tm_p, tm_s = 512, dec_batch * dec_seq
    kv_p = [[] for _ in range(4)]
    kv_s = [[] for _ in range(4)]
    for l in range(depth):
        lam_init = _lambda_init(l)
        row = lambda a: a[l].reshape(1, -1)
        lam_params = [row(lambda_q1), row(lambda_k1), row(lambda_q2), row(lambda_k2)]
        w_in_b, w_o_b = w_in[l].astype(BF16), w_o[l].astype(BF16)
        wg_b, wu_b, wd_b = w_gate[l].astype(BF16), w_up[l].astype(BF16), w_down[l].astype(BF16)

        mq, mk, mv, dq, dk, dv = _proj(xp, row(g_pre_attn), w_in_b, tm_p)
        seq3 = lambda a: a.reshape(batch, seq, -1)
        om = _moba_prompt(seq3(mq), seq3(mk), seq3(mv), s_moba, h_moba)
        od = _diff_prompt(seq3(dq), seq3(dk), seq3(dv), s_diff, lam_params, row(g_subln), lam_init, h_diff)
        x1, h2 = _oproj(om.reshape(batch * seq, wm), od.reshape(batch * seq, wd_), xp, w_o_b,
                        row(g_post_attn), row(g_pre_ffn), tm_p)
        xp = _ffn(h2, x1, wg_b, wu_b, wd_b, row(g_post_ffn), tm_p, 512)
        for dst, a in zip(kv_p, (mk, mv, dk, dv)):
            dst.append(a.reshape(batch, seq, -1, HEAD_DIM))

        mq, mk, mv, dq, dk, dv = _proj(xs, row(g_pre_attn), w_in_b, tm_s)
        dec3 = lambda a: a.reshape(dec_batch, dec_seq, -1)
        om = _sample_attention(
            functools.partial(_moba_sample_kernel, past_len=past_len, n_heads=h_moba),
            [lane_slope_m], _block_diag_q(dec3(mq), h_moba).astype(BF16), dec3(mk), dec3(mv),
            pools[0], pools[1], page_table, l, past_len // MOBA_BLOCK + 8)
        od = _sample_attention(
            functools.partial(_diff_sample_kernel, past_len=past_len, n_heads=h_diff, lam_init=lam_init),
            [lane_slope_d] + lam_params + [row(g_subln)],
            _block_diag_q(dec3(dq) * ((HEAD_DIM // 2) ** -0.5), 2 * h_diff).astype(BF16), dec3(dk), dec3(dv),
            pools[2], pools[3], page_table, l, 0)
        x1, h2 = _oproj(om.reshape(tm_s, wm), od.reshape(tm_s, wd_), xs, w_o_b,
                        row(g_post_attn), row(g_pre_ffn), tm_s)
        xs = _ffn(h2, x1, wg_b, wu_b, wd_b, row(g_post_ffn), tm_s, 512)
        for dst, a in zip(kv_s, (mk, mv, dk, dv)):
            dst.append(a.reshape(dec_batch, dec_seq, -1, HEAD_DIM))

    return (xp.reshape(batch, seq, d_model), xs.reshape(dec_batch, dec_seq, d_model),
            *[jnp.stack(t) for t in kv_p], *[jnp.stack(t) for t in kv_s])
```

```python
import functools
import math

import jax
import jax.numpy as jnp
from jax import lax
from jax.experimental import pallas as pl
from jax.experimental.pallas import tpu as pltpu

F32 = jnp.float32
BF16 = jnp.bfloat16
NEG_INF = float("-inf")

HEAD_DIM = 128
MOBA_BLOCK = 256
MOBA_TOPK = 3
RMS_EPS = 1e-6
PAGES_PER_STEP = 8
VMEM_LIMIT = 56 * 1024 * 1024


def _alibi_slopes(n_heads):
    return 2.0 ** (-8.0 * jnp.arange(1, n_heads + 1, dtype=F32) / n_heads)


def _lambda_init(layer):
    return 0.8 - 0.6 * math.exp(-0.3 * layer)


def _rms(x, g):
    return (x * lax.rsqrt(jnp.mean(x * x, axis=-1, keepdims=True) + RMS_EPS)) * g


def _dot(a, b):
    return jnp.dot(a, b, preferred_element_type=F32)


def _proj_kernel(x_ref, g_ref, w_ref, *refs):
    outs, h_ref = refs[:-1], refs[-1]
    j = pl.program_id(1)

    @pl.when(j == 0)
    def _():
        h_ref[...] = _rms(x_ref[...], g_ref[...]).astype(BF16)

    for k, o_ref in enumerate(outs):
        @pl.when(j == k)
        def _(o_ref=o_ref):
            o_ref[...] = _dot(h_ref[...], w_ref[...])


def _proj(x, g, w, tm):
    m, d = x.shape
    n_out = w.shape[1] // 1024
    return pl.pallas_call(
        _proj_kernel,
        grid=(m // tm, n_out),
        in_specs=[
            pl.BlockSpec((tm, d), lambda i, j: (i, 0)),
            pl.BlockSpec((1, d), lambda i, j: (0, 0)),
            pl.BlockSpec((d, 1024), lambda i, j: (0, j)),
        ],
        out_specs=[pl.BlockSpec((tm, 1024), lambda i, j: (i, 0)) for _ in range(n_out)],
        out_shape=[jax.ShapeDtypeStruct((m, 1024), F32) for _ in range(n_out)],
        scratch_shapes=[pltpu.VMEM((tm, d), BF16)],
        compiler_params=pltpu.CompilerParams(
            dimension_semantics=("arbitrary", "arbitrary"), vmem_limit_bytes=VMEM_LIMIT),
        name="proj",
    )(x, g, w)


def _local_bias(slope):
    t = lax.broadcasted_iota(jnp.int32, (MOBA_BLOCK, MOBA_BLOCK), 1)
    j = lax.broadcasted_iota(jnp.int32, (MOBA_BLOCK, MOBA_BLOCK), 0)
    d = (t - j).astype(F32)
    b0 = -slope * d
    return b0, jnp.where(d >= 0, b0, NEG_INF)


def _stage_kv(k_ref, v_ref, kb_ref, vt_ref):
    blocks = []
    for n in range(k_ref.shape[0] // MOBA_BLOCK):
        rows = slice(n * MOBA_BLOCK, (n + 1) * MOBA_BLOCK)
        kblk = k_ref[rows, :]
        kb_ref[rows, :] = kblk.astype(BF16)
        vt_ref[:, rows] = jnp.transpose(v_ref[rows, :]).astype(BF16)
        blocks.append(kblk)
    return blocks


def _attend_block(i, qt, kb_ref, vt_ref, scale, b0, b0_diag, offsets):
    nk = (i + 1) * MOBA_BLOCK
    raw = _dot(kb_ref[0:nk, :], qt)
    ts, m = [], None
    for n in range(i + 1):
        t = raw[n * MOBA_BLOCK:(n + 1) * MOBA_BLOCK, :]
        if scale != 1.0:
            t = t * scale
        t = t + (b0_diag if n == i else b0)
        mb = jnp.max(t, axis=0, keepdims=True)
        if n < i:
            mb = mb + offsets[n]
        m = mb if m is None else jnp.maximum(m, mb)
        ts.append(t)
    ps, l = [], None
    for n in range(i + 1):
        p = jnp.exp(ts[n] - (m if n == i else m - offsets[n]))
        ps.append(p.astype(BF16))
        pl_sum = jnp.sum(p, axis=0, keepdims=True)
        l = pl_sum if l is None else l + pl_sum
    return _dot(vt_ref[:, 0:nk], jnp.concatenate(ps, axis=0)) / l


def _moba_prompt_kernel(slopes_ref, q_ref, k_ref, v_ref, o_ref, kb_ref, vt_ref):
    nb = q_ref.shape[0] // MOBA_BLOCK
    slope = slopes_ref[pl.program_id(1)]
    kblocks = _stage_kv(k_ref, v_ref, kb_ref, vt_ref)
    km = jnp.concatenate([jnp.mean(kb, axis=0, keepdims=True) for kb in kblocks], axis=0)
    km_hi = km.astype(BF16)
    km_lo = (km - km_hi.astype(F32)).astype(BF16)
    b0, b0_diag = _local_bias(slope)
    blk = lax.broadcasted_iota(jnp.int32, (nb, MOBA_BLOCK), 0)
    for i in range(nb):
        rows = slice(i * MOBA_BLOCK, (i + 1) * MOBA_BLOCK)
        qt = jnp.transpose(q_ref[rows, :]).astype(BF16)
        offsets = []
        if i > 0:
            gate = _dot(km_hi, qt) + _dot(km_lo, qt)
            cnt = jnp.zeros(gate.shape, F32)
            for m in range(i):
                gm = gate[m:m + 1, :]
                beats = (gm > gate) | ((gm == gate) & (m < blk))
                cnt = cnt + jnp.where(beats, 1.0, 0.0)
            selb = jnp.where((blk < i) & (cnt < MOBA_TOPK), 0.0, NEG_INF)
            offsets = [selb[n:n + 1, :] - slope * float((i - n) * MOBA_BLOCK) for n in range(i)]
        o = _attend_block(i, qt, kb_ref, vt_ref, HEAD_DIM ** -0.5, b0, b0_diag, offsets)
        o_ref[rows, :] = jnp.transpose(o).astype(o_ref.dtype)


def _prompt_call(kern, small_inputs, q, k, v, n_heads, name):
    b, t, _ = q.shape
    spec = pl.BlockSpec((None, t, HEAD_DIM), lambda bi, h: (bi, 0, h))
    small = lambda a: pl.BlockSpec(a.shape, lambda bi, h: (0,) * a.ndim)
    return pl.pallas_call(
        kern,
        grid=(b, n_heads),
        in_specs=[pl.BlockSpec(memory_space=pltpu.SMEM)] + [small(a) for a in small_inputs[1:]]
                 + [spec, spec, spec],
        out_specs=spec,
        out_shape=jax.ShapeDtypeStruct((b, t, n_heads * HEAD_DIM), BF16),
        scratch_shapes=[pltpu.VMEM((t, HEAD_DIM), BF16), pltpu.VMEM((HEAD_DIM, t), BF16)],
        compiler_params=pltpu.CompilerParams(
            dimension_semantics=("arbitrary", "arbitrary"), vmem_limit_bytes=VMEM_LIMIT),
        name=name,
    )(*small_inputs, q, k, v)


def _lambda(lq1_ref, lk1_ref, lq2_ref, lk2_ref, lam_init):
    a = jnp.sum(lq1_ref[...] * lk1_ref[...], axis=-1, keepdims=True)
    b = jnp.sum(lq2_ref[...] * lk2_ref[...], axis=-1, keepdims=True)
    return jnp.exp(a) - jnp.exp(b) + lam_init


def _diff_prompt_kernel(slopes_ref, lq1_ref, lk1_ref, lq2_ref, lk2_ref, gsub_ref, q_ref, k_ref, v_ref,
                        o_ref, kb_ref, vt_ref, *, lam_init):
    nb = q_ref.shape[0] // MOBA_BLOCK
    qk_dim = HEAD_DIM // 2
    slope = slopes_ref[pl.program_id(1)]
    lam = _lambda(lq1_ref, lk1_ref, lq2_ref, lk2_ref, lam_init)
    _stage_kv(k_ref, v_ref, kb_ref, vt_ref)
    b0, b0_diag = _local_bias(slope)
    b0 = jnp.concatenate([b0, b0], axis=1)
    b0_diag = jnp.concatenate([b0_diag, b0_diag], axis=1)
    row = lax.broadcasted_iota(jnp.int32, (HEAD_DIM, MOBA_BLOCK), 0)
    for i in range(nb):
        rows = slice(i * MOBA_BLOCK, (i + 1) * MOBA_BLOCK)
        qt = jnp.transpose(q_ref[rows, :]) * (qk_dim ** -0.5)
        qt2 = jnp.concatenate([jnp.where(row < qk_dim, qt, 0.0), jnp.where(row >= qk_dim, qt, 0.0)],
                              axis=1).astype(BF16)
        offsets = [-slope * float((i - n) * MOBA_BLOCK) for n in range(i)]
        on = _attend_block(i, qt2, kb_ref, vt_ref, 1.0, b0, b0_diag, offsets)
        o = jnp.transpose(on[:, :MOBA_BLOCK] - lam * on[:, MOBA_BLOCK:])
        o_ref[rows, :] = (_rms(o, gsub_ref[...]) * (1.0 - lam_init)).astype(o_ref.dtype)


def _page_rows(ref, n_heads):
    keys = ref.shape[0] // n_heads
    return jnp.concatenate(
        [ref[pl.ds(h, keys, stride=n_heads), :].astype(BF16) for h in range(n_heads)], axis=1)


def _sample_common(j, n_ksteps, n_heads, qbd_ref, knew_ref, k_refs, s_ref):
    page = k_refs[0].shape[0] // n_heads

    @pl.when(j < n_ksteps)
    def _():
        for g, k_ref in enumerate(k_refs):
            row0 = pl.multiple_of((j * len(k_refs) + g) * page, page)
            s_ref[pl.ds(row0, page), :] = _dot(_page_rows(k_ref, n_heads), qbd_ref[...])

    @pl.when(j == n_ksteps - 1)
    def _():
        past = n_ksteps * len(k_refs) * page
        knew = knew_ref[...]
        pad = jnp.zeros((page - knew.shape[0], knew.shape[1]), knew.dtype)
        s_ref[past:past + page, :] = _dot(jnp.concatenate([knew, pad], axis=0).astype(BF16), qbd_ref[...])


def _softmax_in_place(s_ref, rl_ref, n_chunks, chunk, logits_of):
    lanes = s_ref.shape[1]

    def pass1(c, m):
        r0 = pl.multiple_of(c * chunk, chunk)
        s = logits_of(c, s_ref[pl.ds(r0, chunk), :])
        s_ref[pl.ds(r0, chunk), :] = s
        return jnp.maximum(m, jnp.max(s, axis=0, keepdims=True))

    m = lax.fori_loop(0, n_chunks, pass1, jnp.full((1, lanes), NEG_INF, F32))

    def pass2(c, l):
        r0 = pl.multiple_of(c * chunk, chunk)
        p = jnp.exp(s_ref[pl.ds(r0, chunk), :] - m)
        s_ref[pl.ds(r0, chunk), :] = p
        return l + jnp.sum(p, axis=0, keepdims=True)

    l = lax.fori_loop(0, n_chunks, pass2, jnp.zeros((1, lanes), F32))
    rl_ref[...] = jnp.broadcast_to(1.0 / l, rl_ref.shape)


def _pv_sweep(j, n_ksteps, n_heads, vnew_ref, v_refs, s_ref, rl_ref, acc_ref):
    page = v_refs[0].shape[0] // n_heads
    rows = page * len(v_refs)

    @pl.when(j == n_ksteps)
    def _():
        acc_ref[...] = jnp.zeros(acc_ref.shape, F32)

    @pl.when(j >= n_ksteps)
    def _():
        r0 = pl.multiple_of((j - n_ksteps) * rows, rows)
        pt = jnp.transpose(s_ref[pl.ds(r0, rows), :] * rl_ref[0:1, :]).astype(BF16)
        vb = jnp.concatenate([_page_rows(v_ref, n_heads) for v_ref in v_refs], axis=0)
        acc_ref[...] += _dot(pt, vb)

    @pl.when(j == 2 * n_ksteps - 1)
    def _():
        past = n_ksteps * rows
        pt = jnp.transpose(s_ref[past:past + page, :] * rl_ref[0:1, :]).astype(BF16)
        vnew = vnew_ref[...]
        pad = jnp.zeros((page - vnew.shape[0], vnew.shape[1]), vnew.dtype)
        acc_ref[...] += _dot(pt, jnp.concatenate([vnew, pad], axis=0).astype(BF16))


def _moba_sample_kernel(pt_ref, slope_ref, qbd_ref, knew_ref, vnew_ref, *refs, past_len, n_heads):
    del pt_ref
    g = PAGES_PER_STEP
    k_refs, v_refs = refs[:g], refs[g:2 * g]
    o_ref, s_ref, acc_ref, rl_ref, selb_ref = refs[2 * g:]
    page = k_refs[0].shape[0] // n_heads
    dec = knew_ref.shape[0]
    n_ksteps = past_len // (page * g)
    n_past_blk = past_len // MOBA_BLOCK
    j = pl.program_id(1)
    _sample_common(j, n_ksteps, n_heads, qbd_ref, knew_ref, k_refs, s_ref)

    @pl.when(j == n_ksteps - 1)
    def _():
        lanes = s_ref.shape[1]

        def gate_row(n, carry):
            r0 = pl.multiple_of(n * MOBA_BLOCK, MOBA_BLOCK)
            selb_ref[pl.ds(n, 1), :] = jnp.sum(s_ref[pl.ds(r0, MOBA_BLOCK), :], axis=0, keepdims=True) \
                * (1.0 / MOBA_BLOCK)
            return carry

        lax.fori_loop(0, n_past_blk, gate_row, 0)
        gate = selb_ref[0:n_past_blk, :]
        blk = lax.broadcasted_iota(jnp.int32, gate.shape, 0)
        cnt = jnp.zeros(gate.shape, F32)
        for m in range(n_past_blk):
            gm = gate[m:m + 1, :]
            beats = (gm > gate) | ((gm == gate) & (m < blk))
            cnt = cnt + jnp.where(beats, 1.0, 0.0)
        selb_ref[0:n_past_blk, :] = jnp.where(cnt < MOBA_TOPK, 0.0, NEG_INF)
        selb_ref[n_past_blk:n_past_blk + 8, :] = jnp.zeros((8, lanes), F32)

        lane = lax.broadcasted_iota(jnp.int32, (1, lanes), 1)
        q_pos = past_len + lane % dec
        slope = slope_ref[...]
        scale = HEAD_DIM ** -0.5

        def logits_of(c, raw):
            k_pos = c * page + lax.broadcasted_iota(jnp.int32, (page, 1), 0)
            dist = q_pos - k_pos
            sb = selb_ref[pl.ds(c * page // MOBA_BLOCK, 1), :]
            s = raw * scale - slope * dist.astype(F32) + sb
            return jnp.where(dist >= 0, s, NEG_INF)

        _softmax_in_place(s_ref, rl_ref, s_ref.shape[0] // page, page, logits_of)

    _pv_sweep(j, n_ksteps, n_heads, vnew_ref, v_refs, s_ref, rl_ref, acc_ref)

    @pl.when(j == 2 * n_ksteps - 1)
    def _():
        acc = acc_ref[...]
        for h in range(n_heads):
            o_ref[:, h * HEAD_DIM:(h + 1) * HEAD_DIM] = \
                acc[h * dec:(h + 1) * dec, h * HEAD_DIM:(h + 1) * HEAD_DIM].astype(o_ref.dtype)


def _diff_sample_kernel(pt_ref, slope_ref, lq1_ref, lk1_ref, lq2_ref, lk2_ref, gsub_ref, qbd_ref, knew_ref,
                        vnew_ref, *refs, past_len, n_heads, lam_init):
    del pt_ref
    g = PAGES_PER_STEP
    k_refs, v_refs = refs[:g], refs[g:2 * g]
    o_ref, s_ref, acc_ref, rl_ref = refs[2 * g:]
    page = k_refs[0].shape[0] // n_heads
    dec = knew_ref.shape[0]
    n_ksteps = past_len // (page * g)
    j = pl.program_id(1)
    _sample_common(j, n_ksteps, n_heads, qbd_ref, knew_ref, k_refs, s_ref)

    @pl.when(j == n_ksteps - 1)
    def _():
        lanes = s_ref.shape[1]
        lane = lax.broadcasted_iota(jnp.int32, (1, lanes), 1)
        q_pos = past_len + lane % dec
        slope = slope_ref[...]

        def logits_of(c, raw):
            k_pos = c * page + lax.broadcasted_iota(jnp.int32, (page, 1), 0)
            dist = q_pos - k_pos
            return jnp.where(dist >= 0, raw - slope * dist.astype(F32), NEG_INF)

        _softmax_in_place(s_ref, rl_ref, s_ref.shape[0] // page, page, logits_of)

    _pv_sweep(j, n_ksteps, n_heads, vnew_ref, v_refs, s_ref, rl_ref, acc_ref)

    @pl.when(j == 2 * n_ksteps - 1)
    def _():
        acc = acc_ref[...]
        lam = _lambda(lq1_ref, lk1_ref, lq2_ref, lk2_ref, lam_init)
        for h in range(n_heads):
            cols = slice(h * HEAD_DIM, (h + 1) * HEAD_DIM)
            r = 2 * h * dec
            o = acc[r:r + dec, cols] - lam * acc[r + dec:r + 2 * dec, cols]
            o_ref[:, cols] = (_rms(o, gsub_ref[...]) * (1.0 - lam_init)).astype(o_ref.dtype)


def _sample_attention(kern, extra_inputs, qbd, knew, vnew, pool_k, pool_v, page_table, layer, n_scratch_sel):
    nb, n_pages = page_table.shape
    page_rows, hd = pool_k.shape[2], pool_k.shape[3]
    dec, width = knew.shape[1], knew.shape[2]
    page = page_rows * hd // width
    g = PAGES_PER_STEP
    n_ksteps = n_pages // g
    lanes = qbd.shape[2]

    def k_spec(gi):
        return pl.BlockSpec((None, None, page_rows, hd),
                            lambda b, j, pt: (layer, pt[b, jnp.minimum(j, n_ksteps - 1) * g + gi], 0, 0))

    def v_spec(gi):
        return pl.BlockSpec((None, None, page_rows, hd),
                            lambda b, j, pt: (layer, pt[b, jnp.maximum(j - n_ksteps, 0) * g + gi], 0, 0))

    small = lambda a: pl.BlockSpec(a.shape, lambda b, j, pt: (0,) * a.ndim)
    per_seq = lambda a: pl.BlockSpec((None,) + a.shape[1:], lambda b, j, pt: (b,) + (0,) * (a.ndim - 1))
    scratch = [
        pltpu.VMEM((n_pages * page + page, lanes), F32),
        pltpu.VMEM((lanes, width), F32),
        pltpu.VMEM((8, lanes), F32),
    ]
    if n_scratch_sel:
        scratch.append(pltpu.VMEM((n_scratch_sel, lanes), F32))
    return pl.pallas_call(
        kern,
        grid_spec=pltpu.PrefetchScalarGridSpec(
            num_scalar_prefetch=1,
            grid=(nb, 2 * n_ksteps),
            in_specs=[small(a) for a in extra_inputs] + [per_seq(qbd), per_seq(knew), per_seq(vnew)]
                     + [k_spec(gi) for gi in range(g)] + [v_spec(gi) for gi in range(g)],
            out_specs=pl.BlockSpec((None, dec, width), lambda b, j, pt: (b, 0, 0)),
            scratch_shapes=scratch,
        ),
        out_shape=jax.ShapeDtypeStruct((nb, dec, width), F32),
        compiler_params=pltpu.CompilerParams(
            dimension_semantics=("arbitrary", "arbitrary"), vmem_limit_bytes=VMEM_LIMIT),
        name=kern.func.__name__.strip("_"),
    )(page_table, *extra_inputs, qbd, knew, vnew, *([pool_k] * g), *([pool_v] * g))


def _block_diag_q(q, groups):
    b, dec, width = q.shape
    gd = width // groups
    t = q.reshape(b, dec, groups, gd).transpose(0, 2, 3, 1)
    eye = jnp.eye(groups, dtype=q.dtype)
    out = t[:, :, :, None, :] * eye[None, :, None, :, None]
    out = out.reshape(b, width, groups * dec)
    return jnp.pad(out, ((0, 0), (0, 0), (0, 128 - groups * dec)))


def _oproj_kernel(om_ref, od_ref, x_ref, w_ref, gpost_ref, gpre_ref, x1_ref, h2_ref):
    o = jnp.concatenate([om_ref[...].astype(BF16), od_ref[...].astype(BF16)], axis=1)
    x1 = x_ref[...] + _rms(_dot(o, w_ref[...]), gpost_ref[...])
    x1_ref[...] = x1
    h2_ref[...] = _rms(x1, gpre_ref[...]).astype(BF16)


def _oproj(om, od, x, w, gpost, gpre, tm):
    m, d = x.shape
    row = lambda width: pl.BlockSpec((tm, width), lambda i: (i, 0))
    const = lambda a: pl.BlockSpec(a.shape, lambda i: (0, 0))
    return pl.pallas_call(
        _oproj_kernel,
        grid=(m // tm,),
        in_specs=[row(om.shape[1]), row(od.shape[1]), row(d), const(w), const(gpost), const(gpre)],
        out_specs=[row(d), row(d)],
        out_shape=[jax.ShapeDtypeStruct((m, d), F32), jax.ShapeDtypeStruct((m, d), BF16)],
        compiler_params=pltpu.CompilerParams(
            dimension_semantics=("arbitrary",), vmem_limit_bytes=VMEM_LIMIT),
        name="oproj",
    )(om, od, x, w, gpost, gpre)


def _ffn_kernel(h_ref, x_ref, wg_ref, wu_ref, wd_ref, g_ref, o_ref, acc_ref):
    j = pl.program_id(1)
    h = h_ref[...]
    gate = _dot(h, wg_ref[...])
    up = _dot(h, wu_ref[...])
    a = (gate * (1.0 / (1.0 + jnp.exp(-gate))) * up).astype(BF16)
    part = _dot(a, wd_ref[...])

    @pl.when(j == 0)
    def _():
        acc_ref[...] = part

    @pl.when(j > 0)
    def _():
        acc_ref[...] += part

    @pl.when(j == pl.num_programs(1) - 1)
    def _():
        o_ref[...] = x_ref[...] + _rms(acc_ref[...], g_ref[...])


def _ffn(h, x, wg, wu, wd, g, tm, tf):
    m, d = x.shape
    f = wg.shape[1]
    return pl.pallas_call(
        _ffn_kernel,
        grid=(m // tm, f // tf),
        in_specs=[
            pl.BlockSpec((tm, d), lambda i, j: (i, 0)),
            pl.BlockSpec((tm, d), lambda i, j: (i, 0)),
            pl.BlockSpec((d, tf), lambda i, j: (0, j)),
            pl.BlockSpec((d, tf), lambda i, j: (0, j)),
            pl.BlockSpec((tf, d), lambda i, j: (j, 0)),
            pl.BlockSpec((1, d), lambda i, j: (0, 0)),
        ],
        out_specs=pl.BlockSpec((tm, d), lambda i, j: (i, 0)),
        out_shape=jax.ShapeDtypeStruct((m, d), F32),
        scratch_shapes=[pltpu.VMEM((tm, d), F32)],
        compiler_params=pltpu.CompilerParams(
            dimension_semantics=("arbitrary", "arbitrary"), vmem_limit_bytes=VMEM_LIMIT),
        name="ffn",
    )(h, x, wg, wu, wd, g)


def kernel(x_prompt, x_sample, cache_moba_k, cache_moba_v, cache_diff_k, cache_diff_v, page_table, g_pre_attn, w_in, lambda_q1, lambda_k1, lambda_q2, lambda_k2, g_subln, w_o, g_post_attn, g_pre_ffn, w_gate, w_up, w_down, g_post_ffn):
    batch, seq, d_model = x_prompt.shape
    dec_batch, dec_seq, _ = x_sample.shape
    depth, n_pool, page, h_moba, _ = cache_moba_k.shape
    h_diff = cache_diff_k.shape[3]
    n_heads = h_moba + h_diff
    wm, wd_ = h_moba * HEAD_DIM, h_diff * HEAD_DIM
    past_len = page_table.shape[1] * page
    assert seq % MOBA_BLOCK == 0 and past_len % MOBA_BLOCK == 0 and dec_seq <= MOBA_BLOCK
    assert wm == 1024 and wd_ == 1024 and h_moba * dec_seq <= 128 and 2 * h_diff * dec_seq <= 128
    assert cache_diff_k.shape[4] == HEAD_DIM and cache_diff_v.shape[4] == HEAD_DIM

    slopes = _alibi_slopes(n_heads)
    s_moba, s_diff = slopes[0::2], slopes[1::2]
    pad128 = lambda v: jnp.pad(v, (0, 128 - v.shape[0])).reshape(1, 128)
    lane_slope_m = pad128(jnp.repeat(s_moba, dec_seq))
    lane_slope_d = pad128(jnp.repeat(s_diff, 2 * dec_seq))
    pools = [c.reshape(depth, n_pool, -1, HEAD_DIM) for c in (cache_moba_k, cache_moba_v, cache_diff_k, cache_diff_v)]

    xp = x_prompt.reshape(batch * seq, d_model)
    xs = x_sample.reshape(dec_batch * dec_seq, d_model)
    tm_p, tm_s = 512, dec_batch * dec_seq
    kv_p = [[] for _ in range(4)]
    kv_s = [[] for _ in range(4)]
    for l in range(depth):
        lam_init = _lambda_init(l)
        row = lambda a: a[l].reshape(1, -1)
        lam_params = [row(lambda_q1), row(lambda_k1), row(lambda_q2), row(lambda_k2)]
        w_in_b, w_o_b = w_in[l].astype(BF16), w_o[l].astype(BF16)
        wg_b, wu_b, wd_b = w_gate[l].astype(BF16), w_up[l].astype(BF16), w_down[l].astype(BF16)

        mq, mk, mv, dq, dk, dv = _proj(xp, row(g_pre_attn), w_in_b, tm_p)
        seq3 = lambda a: a.reshape(batch, seq, -1)
        om = _prompt_call(_moba_prompt_kernel, [s_moba], seq3(mq), seq3(mk), seq3(mv), h_moba, "moba_prompt")
        od = _prompt_call(functools.partial(_diff_prompt_kernel, lam_init=lam_init),
                          [s_diff] + lam_params + [row(g_subln)], seq3(dq), seq3(dk), seq3(dv), h_diff,
                          "diff_prompt")
        x1, h2 = _oproj(om.reshape(batch * seq, wm), od.reshape(batch * seq, wd_), xp, w_o_b,
                        row(g_post_attn), row(g_pre_ffn), tm_p)
        xp = _ffn(h2, x1, wg_b, wu_b, wd_b, row(g_post_ffn), tm_p, 512)
        for dst, a in zip(kv_p, (mk, mv, dk, dv)):
            dst.append(a.reshape(batch, seq, -1, HEAD_DIM))

        mq, mk, mv, dq, dk, dv = _proj(xs, row(g_pre_attn), w_in_b, tm_s)
        dec3 = lambda a: a.reshape(dec_batch, dec_seq, -1)
        om = _sample_attention(
            functools.partial(_moba_sample_kernel, past_len=past_len, n_heads=h_moba),
            [lane_slope_m], _block_diag_q(dec3(mq), h_moba).astype(BF16), dec3(mk), dec3(mv),
            pools[0], pools[1], page_table, l, past_len // MOBA_BLOCK + 8)
        od = _sample_attention(
            functools.partial(_diff_sample_kernel, past_len=past_len, n_heads=h_diff, lam_init=lam_init),
            [lane_slope_d] + lam_params + [row(g_subln)],
            _block_diag_q(dec3(dq) * ((HEAD_DIM // 2) ** -0.5), 2 * h_diff).astype(BF16), dec3(dk), dec3(dv),
            pools[2], pools[3], page_table, l, 0)
        x1, h2 = _oproj(om.reshape(tm_s, wm), od.reshape(tm_s, wd_), xs, w_o_b,
                        row(g_post_attn), row(g_pre_ffn), tm_s)
        xs = _ffn(h2, x1, wg_b, wu_b, wd_b, row(g_post_ffn), tm_s, 512)
        for dst, a in zip(kv_s, (mk, mv, dk, dv)):
            dst.append(a.reshape(dec_batch, dec_seq, -1, HEAD_DIM))

    return (xp.reshape(batch, seq, d_model), xs.reshape(dec_batch, dec_seq, d_model),
            *[jnp.stack(t) for t in kv_p], *[jnp.stack(t) for t in kv_s])
```

```python
import functools
import math

import jax
import jax.numpy as jnp
from jax import lax
from jax.experimental import pallas as pl
from jax.experimental.pallas import tpu as pltpu

F32 = jnp.float32
BF16 = jnp.bfloat16
NEG_INF = float("-inf")

HEAD_DIM = 128
GROUP_WIDTH = 1024
MOBA_BLOCK = 256
MOBA_TOPK = 3
RMS_EPS = 1e-6
PAGES_PER_STEP = 8
RING_SLOTS = 4
VMEM_LIMIT = 56 * 1024 * 1024


def _alibi_slopes(n_heads):
    return 2.0 ** (-8.0 * jnp.arange(1, n_heads + 1, dtype=F32) / n_heads)


def _lambda_init(layer):
    return 0.8 - 0.6 * math.exp(-0.3 * layer)


def _rms(x, g):
    return (x * lax.rsqrt(jnp.mean(x * x, axis=-1, keepdims=True) + RMS_EPS)) * g


def _dot(a, b):
    return jnp.dot(a, b, preferred_element_type=F32)


def _proj_kernel(x_ref, g_ref, w_ref, *refs):
    outs, h_ref = refs[:-1], refs[-1]
    j = pl.program_id(1)

    @pl.when(j == 0)
    def _():
        h_ref[...] = _rms(x_ref[...], g_ref[...]).astype(BF16)

    per_out = GROUP_WIDTH // outs[0].shape[1]
    for k, o_ref in enumerate(outs):
        @pl.when((j >= k * per_out) & (j < (k + 1) * per_out))
        def _(o_ref=o_ref):
            o_ref[...] = _dot(h_ref[...], w_ref[...])


def _proj(x, g, w, tm, tn):
    m, d = x.shape
    n_out = w.shape[1] // GROUP_WIDTH
    per_out = GROUP_WIDTH // tn
    out_map = lambda k: (lambda i, j: (i, jnp.clip(j - k * per_out, 0, per_out - 1)))
    return pl.pallas_call(
        _proj_kernel,
        grid=(m // tm, n_out * per_out),
        in_specs=[
            pl.BlockSpec((tm, d), lambda i, j: (i, 0)),
            pl.BlockSpec((1, d), lambda i, j: (0, 0)),
            pl.BlockSpec((d, tn), lambda i, j: (0, j)),
        ],
        out_specs=[pl.BlockSpec((tm, tn), out_map(k)) for k in range(n_out)],
        out_shape=[jax.ShapeDtypeStruct((m, GROUP_WIDTH), F32) for _ in range(n_out)],
        scratch_shapes=[pltpu.VMEM((tm, d), BF16)],
        compiler_params=pltpu.CompilerParams(
            dimension_semantics=("arbitrary", "arbitrary"), vmem_limit_bytes=VMEM_LIMIT),
        name="proj",
    )(x, g, w)


def _local_bias(slope):
    t = lax.broadcasted_iota(jnp.int32, (MOBA_BLOCK, MOBA_BLOCK), 1)
    j = lax.broadcasted_iota(jnp.int32, (MOBA_BLOCK, MOBA_BLOCK), 0)
    d = (t - j).astype(F32)
    b0 = -slope * d
    return b0, jnp.where(d >= 0, b0, NEG_INF)


def _stage_kv(k_ref, v_ref, kb_ref, vt_ref):
    blocks = []
    for n in range(k_ref.shape[0] // MOBA_BLOCK):
        rows = slice(n * MOBA_BLOCK, (n + 1) * MOBA_BLOCK)
        kblk = k_ref[rows, :]
        kb_ref[rows, :] = kblk.astype(BF16)
        vt_ref[:, rows] = jnp.transpose(v_ref[rows, :]).astype(BF16)
        blocks.append(kblk)
    return blocks


def _attend_block(i, qt, kb_ref, vt_ref, scale, b0, b0_diag, offsets):
    nk = (i + 1) * MOBA_BLOCK
    raw = _dot(kb_ref[0:nk, :], qt)
    ts, m = [], None
    for n in range(i + 1):
        t = raw[n * MOBA_BLOCK:(n + 1) * MOBA_BLOCK, :]
        if scale != 1.0:
            t = t * scale
        t = t + (b0_diag if n == i else b0)
        mb = jnp.max(t, axis=0, keepdims=True)
        if n < i:
            mb = mb + offsets[n]
        m = mb if m is None else jnp.maximum(m, mb)
        ts.append(t)
    ps, l = [], None
    for n in range(i + 1):
        p = jnp.exp(ts[n] - (m if n == i else m - offsets[n]))
        ps.append(p.astype(BF16))
        pl_sum = jnp.sum(p, axis=0, keepdims=True)
        l = pl_sum if l is None else l + pl_sum
    return _dot(vt_ref[:, 0:nk], jnp.concatenate(ps, axis=0)) / l


def _moba_prompt_kernel(slopes_ref, q_ref, k_ref, v_ref, o_ref, kb_ref, vt_ref):
    nb = q_ref.shape[0] // MOBA_BLOCK
    slope = slopes_ref[pl.program_id(1)]
    kblocks = _stage_kv(k_ref, v_ref, kb_ref, vt_ref)
    km = jnp.concatenate([jnp.mean(kb, axis=0, keepdims=True) for kb in kblocks], axis=0)
    km_hi = km.astype(BF16)
    km_lo = (km - km_hi.astype(F32)).astype(BF16)
    b0, b0_diag = _local_bias(slope)
    blk = lax.broadcasted_iota(jnp.int32, (nb, MOBA_BLOCK), 0)
    for i in range(nb):
        rows = slice(i * MOBA_BLOCK, (i + 1) * MOBA_BLOCK)
        qt = jnp.transpose(q_ref[rows, :]).astype(BF16)
        offsets = []
        if i > 0:
            gate = _dot(km_hi, qt) + _dot(km_lo, qt)
            cnt = jnp.zeros(gate.shape, F32)
            for m in range(i):
                gm = gate[m:m + 1, :]
                beats = (gm > gate) | ((gm == gate) & (m < blk))
                cnt = cnt + jnp.where(beats, 1.0, 0.0)
            selb = jnp.where((blk < i) & (cnt < MOBA_TOPK), 0.0, NEG_INF)
            offsets = [selb[n:n + 1, :] - slope * float((i - n) * MOBA_BLOCK) for n in range(i)]
        o = _attend_block(i, qt, kb_ref, vt_ref, HEAD_DIM ** -0.5, b0, b0_diag, offsets)
        o_ref[rows, :] = jnp.transpose(o).astype(o_ref.dtype)


def _prompt_call(kern, small_inputs, q, k, v, n_heads, name):
    b, t, _ = q.shape
    spec = pl.BlockSpec((None, t, HEAD_DIM), lambda bi, h: (bi, 0, h))
    small = lambda a: pl.BlockSpec(a.shape, lambda bi, h: (0,) * a.ndim)
    return pl.pallas_call(
        kern,
        grid=(b, n_heads),
        in_specs=[pl.BlockSpec(memory_space=pltpu.SMEM)] + [small(a) for a in small_inputs[1:]]
                 + [spec, spec, spec],
        out_specs=spec,
        out_shape=jax.ShapeDtypeStruct((b, t, n_heads * HEAD_DIM), BF16),
        scratch_shapes=[pltpu.VMEM((t, HEAD_DIM), BF16), pltpu.VMEM((HEAD_DIM, t), BF16)],
        compiler_params=pltpu.CompilerParams(
            dimension_semantics=("arbitrary", "arbitrary"), vmem_limit_bytes=VMEM_LIMIT),
        name=name,
    )(*small_inputs, q, k, v)


def _lambda(lq1_ref, lk1_ref, lq2_ref, lk2_ref, lam_init):
    a = jnp.sum(lq1_ref[...] * lk1_ref[...], axis=-1, keepdims=True)
    b = jnp.sum(lq2_ref[...] * lk2_ref[...], axis=-1, keepdims=True)
    return jnp.exp(a) - jnp.exp(b) + lam_init


def _diff_prompt_kernel(slopes_ref, lq1_ref, lk1_ref, lq2_ref, lk2_ref, gsub_ref, q_ref, k_ref, v_ref,
                        o_ref, kb_ref, vt_ref, *, lam_init):
    nb = q_ref.shape[0] // MOBA_BLOCK
    qk_dim = HEAD_DIM // 2
    slope = slopes_ref[pl.program_id(1)]
    lam = _lambda(lq1_ref, lk1_ref, lq2_ref, lk2_ref, lam_init)
    _stage_kv(k_ref, v_ref, kb_ref, vt_ref)
    b0, b0_diag = _local_bias(slope)
    b0 = jnp.concatenate([b0, b0], axis=1)
    b0_diag = jnp.concatenate([b0_diag, b0_diag], axis=1)
    row = lax.broadcasted_iota(jnp.int32, (HEAD_DIM, MOBA_BLOCK), 0)
    for i in range(nb):
        rows = slice(i * MOBA_BLOCK, (i + 1) * MOBA_BLOCK)
        qt = jnp.transpose(q_ref[rows, :]) * (qk_dim ** -0.5)
        qt2 = jnp.concatenate([jnp.where(row < qk_dim, qt, 0.0), jnp.where(row >= qk_dim, qt, 0.0)],
                              axis=1).astype(BF16)
        offsets = [-slope * float((i - n) * MOBA_BLOCK) for n in range(i)]
        on = _attend_block(i, qt2, kb_ref, vt_ref, 1.0, b0, b0_diag, offsets)
        o = jnp.transpose(on[:, :MOBA_BLOCK] - lam * on[:, MOBA_BLOCK:])
        o_ref[rows, :] = (_rms(o, gsub_ref[...]) * (1.0 - lam_init)).astype(o_ref.dtype)


class _PageRing:
    def __init__(self, pt_ref, kpool, vpool, ring, sem, layer, n_ksteps, n_heads):
        self.pt_ref, self.kpool, self.vpool, self.ring, self.sem = pt_ref, kpool, vpool, ring, sem
        self.layer, self.n_ksteps, self.n_heads = layer, n_ksteps, n_heads
        self.page_rows = kpool.shape[2]
        self.keys = self.page_rows // n_heads
        nstep = 2 * n_ksteps
        self.t = pl.program_id(0) * nstep + pl.program_id(1)
        self.total = pl.num_programs(0) * nstep
        self.slot = lax.rem(self.t, RING_SLOTS)

    def _copy(self, pool, page_idx, slot, gi):
        return pltpu.make_async_copy(pool.at[self.layer, page_idx],
                                     self.ring.at[slot, pl.ds(gi * self.page_rows, self.page_rows)],
                                     self.sem.at[slot])

    def _start(self, tt):
        nstep = 2 * self.n_ksteps
        b2, j2 = lax.div(tt, nstep), lax.rem(tt, nstep)
        slot = lax.rem(tt, RING_SLOTS)
        for pool, cond, first in ((self.kpool, j2 < self.n_ksteps, j2 * PAGES_PER_STEP),
                                  (self.vpool, j2 >= self.n_ksteps, (j2 - self.n_ksteps) * PAGES_PER_STEP)):
            @pl.when(cond)
            def _(pool=pool, first=first):
                for gi in range(PAGES_PER_STEP):
                    self._copy(pool, self.pt_ref[b2, first + gi], slot, gi).start()

    def advance(self):
        ahead = RING_SLOTS - 1

        @pl.when(self.t == 0)
        def _():
            for d in range(ahead):
                self._start(jnp.int32(d))

        @pl.when(self.t + ahead < self.total)
        def _():
            self._start(self.t + ahead)

        for gi in range(PAGES_PER_STEP):
            self._copy(self.kpool, 0, self.slot, gi).wait()

    def page(self, gi):
        return jnp.concatenate(
            [self.ring[self.slot, pl.ds(gi * self.page_rows + h, self.keys, stride=self.n_heads), :].astype(BF16)
             for h in range(self.n_heads)], axis=1)


def _sample_common(j, n_ksteps, pages, qbd_ref, knew_ref, s_ref):
    page = pages.keys

    @pl.when(j < n_ksteps)
    def _():
        for g in range(PAGES_PER_STEP):
            row0 = pl.multiple_of((j * PAGES_PER_STEP + g) * page, page)
            s_ref[pl.ds(row0, page), :] = _dot(pages.page(g), qbd_ref[...])

    @pl.when(j == n_ksteps - 1)
    def _():
        past = n_ksteps * PAGES_PER_STEP * page
        knew = knew_ref[...]
        pad = jnp.zeros((page - knew.shape[0], knew.shape[1]), knew.dtype)
        s_ref[past:past + page, :] = _dot(jnp.concatenate([knew, pad], axis=0).astype(BF16), qbd_ref[...])


def _softmax_in_place(s_ref, rl_ref, n_chunks, chunk, logits_of):
    lanes = s_ref.shape[1]

    def pass1(c, m):
        r0 = pl.multiple_of(c * chunk, chunk)
        s = logits_of(c, s_ref[pl.ds(r0, chunk), :])
        s_ref[pl.ds(r0, chunk), :] = s
        return jnp.maximum(m, jnp.max(s, axis=0, keepdims=True))

    m = lax.fori_loop(0, n_chunks, pass1, jnp.full((1, lanes), NEG_INF, F32))

    def pass2(c, l):
        r0 = pl.multiple_of(c * chunk, chunk)
        p = jnp.exp(s_ref[pl.ds(r0, chunk), :] - m)
        s_ref[pl.ds(r0, chunk), :] = p
        return l + jnp.sum(p, axis=0, keepdims=True)

    l = lax.fori_loop(0, n_chunks, pass2, jnp.zeros((1, lanes), F32))
    rl_ref[...] = jnp.broadcast_to(1.0 / l, rl_ref.shape)


def _pv_sweep(j, n_ksteps, pages, vnew_ref, s_ref, rl_ref, acc_ref):
    page = pages.keys
    rows = page * PAGES_PER_STEP

    @pl.when(j == n_ksteps)
    def _():
        acc_ref[...] = jnp.zeros(acc_ref.shape, F32)

    @pl.when(j >= n_ksteps)
    def _():
        r0 = pl.multiple_of((j - n_ksteps) * rows, rows)
        pt = jnp.transpose(s_ref[pl.ds(r0, rows), :] * rl_ref[0:1, :]).astype(BF16)
        vb = jnp.concatenate([pages.page(g) for g in range(PAGES_PER_STEP)], axis=0)
        acc_ref[...] += _dot(pt, vb)

    @pl.when(j == 2 * n_ksteps - 1)
    def _():
        past = n_ksteps * rows
        pt = jnp.transpose(s_ref[past:past + page, :] * rl_ref[0:1, :]).astype(BF16)
        vnew = vnew_ref[...]
        pad = jnp.zeros((page - vnew.shape[0], vnew.shape[1]), vnew.dtype)
        acc_ref[...] += _dot(pt, jnp.concatenate([vnew, pad], axis=0).astype(BF16))


def _moba_sample_kernel(pt_ref, slope_ref, qbd_ref, knew_ref, vnew_ref, kpool, vpool, o_ref,
                        s_ref, acc_ref, rl_ref, ring, sem, selb_ref, *, past_len, n_heads, layer):
    page = kpool.shape[2] // n_heads
    dec = knew_ref.shape[0]
    n_ksteps = past_len // (page * PAGES_PER_STEP)
    n_past_blk = past_len // MOBA_BLOCK
    j = pl.program_id(1)
    pages = _PageRing(pt_ref, kpool, vpool, ring, sem, layer, n_ksteps, n_heads)
    pages.advance()
    _sample_common(j, n_ksteps, pages, qbd_ref, knew_ref, s_ref)

    @pl.when(j == n_ksteps - 1)
    def _():
        lanes = s_ref.shape[1]

        def gate_row(n, carry):
            r0 = pl.multiple_of(n * MOBA_BLOCK, MOBA_BLOCK)
            selb_ref[pl.ds(n, 1), :] = jnp.sum(s_ref[pl.ds(r0, MOBA_BLOCK), :], axis=0, keepdims=True) \
                * (1.0 / MOBA_BLOCK)
            return carry

        lax.fori_loop(0, n_past_blk, gate_row, 0)
        gate = selb_ref[0:n_past_blk, :]
        blk = lax.broadcasted_iota(jnp.int32, gate.shape, 0)
        cnt = jnp.zeros(gate.shape, F32)
        for m in range(n_past_blk):
            gm = gate[m:m + 1, :]
            beats = (gm > gate) | ((gm == gate) & (m < blk))
            cnt = cnt + jnp.where(beats, 1.0, 0.0)
        selb_ref[0:n_past_blk, :] = jnp.where(cnt < MOBA_TOPK, 0.0, NEG_INF)
        selb_ref[n_past_blk:n_past_blk + 8, :] = jnp.zeros((8, lanes), F32)

        lane = lax.broadcasted_iota(jnp.int32, (1, lanes), 1)
        q_pos = past_len + lane % dec
        slope = slope_ref[...]
        scale = HEAD_DIM ** -0.5

        def logits_of(c, raw):
            k_pos = c * page + lax.broadcasted_iota(jnp.int32, (page, 1), 0)
            dist = q_pos - k_pos
            sb = selb_ref[pl.ds(c * page // MOBA_BLOCK, 1), :]
            s = raw * scale - slope * dist.astype(F32) + sb
            return jnp.where(dist >= 0, s, NEG_INF)

        _softmax_in_place(s_ref, rl_ref, s_ref.shape[0] // page, page, logits_of)

    _pv_sweep(j, n_ksteps, pages, vnew_ref, s_ref, rl_ref, acc_ref)

    @pl.when(j == 2 * n_ksteps - 1)
    def _():
        acc = acc_ref[...]
        for h in range(n_heads):
            o_ref[:, h * HEAD_DIM:(h + 1) * HEAD_DIM] = \
                acc[h * dec:(h + 1) * dec, h * HEAD_DIM:(h + 1) * HEAD_DIM].astype(o_ref.dtype)


def _diff_sample_kernel(pt_ref, slope_ref, lq1_ref, lk1_ref, lq2_ref, lk2_ref, gsub_ref, qbd_ref, knew_ref,
                        vnew_ref, kpool, vpool, o_ref, s_ref, acc_ref, rl_ref, ring, sem,
                        *, past_len, n_heads, layer, lam_init):
    page = kpool.shape[2] // n_heads
    dec = knew_ref.shape[0]
    n_ksteps = past_len // (page * PAGES_PER_STEP)
    j = pl.program_id(1)
    pages = _PageRing(pt_ref, kpool, vpool, ring, sem, layer, n_ksteps, n_heads)
    pages.advance()
    _sample_common(j, n_ksteps, pages, qbd_ref, knew_ref, s_ref)

    @pl.when(j == n_ksteps - 1)
    def _():
        lanes = s_ref.shape[1]
        lane = lax.broadcasted_iota(jnp.int32, (1, lanes), 1)
        q_pos = past_len + lane % dec
        slope = slope_ref[...]

        def logits_of(c, raw):
            k_pos = c * page + lax.broadcasted_iota(jnp.int32, (page, 1), 0)
            dist = q_pos - k_pos
            return jnp.where(dist >= 0, raw - slope * dist.astype(F32), NEG_INF)

        _softmax_in_place(s_ref, rl_ref, s_ref.shape[0] // page, page, logits_of)

    _pv_sweep(j, n_ksteps, pages, vnew_ref, s_ref, rl_ref, acc_ref)

    @pl.when(j == 2 * n_ksteps - 1)
    def _():
        acc = acc_ref[...]
        lam = _lambda(lq1_ref, lk1_ref, lq2_ref, lk2_ref, lam_init)
        for h in range(n_heads):
            cols = slice(h * HEAD_DIM, (h + 1) * HEAD_DIM)
            r = 2 * h * dec
            o = acc[r:r + dec, cols] - lam * acc[r + dec:r + 2 * dec, cols]
            o_ref[:, cols] = (_rms(o, gsub_ref[...]) * (1.0 - lam_init)).astype(o_ref.dtype)


def _sample_attention(kern, extra_inputs, qbd, knew, vnew, pool_k, pool_v, page_table, n_scratch_sel):
    nb, n_pages = page_table.shape
    page_rows, hd = pool_k.shape[2], pool_k.shape[3]
    dec, width = knew.shape[1], knew.shape[2]
    page = page_rows * hd // width
    n_ksteps = n_pages // PAGES_PER_STEP
    lanes = qbd.shape[2]
    small = lambda a: pl.BlockSpec(a.shape, lambda b, j, pt: (0,) * a.ndim)
    per_seq = lambda a: pl.BlockSpec((None,) + a.shape[1:], lambda b, j, pt: (b,) + (0,) * (a.ndim - 1))
    hbm = pl.BlockSpec(memory_space=pl.ANY)
    scratch = [
        pltpu.VMEM((n_pages * page + page, lanes), F32),
        pltpu.VMEM((lanes, width), F32),
        pltpu.VMEM((8, lanes), F32),
        pltpu.VMEM((RING_SLOTS, PAGES_PER_STEP * page_rows, hd), pool_k.dtype),
        pltpu.SemaphoreType.DMA((RING_SLOTS,)),
    ]
    if n_scratch_sel:
        scratch.append(pltpu.VMEM((n_scratch_sel, lanes), F32))
    return pl.pallas_call(
        kern,
        grid_spec=pltpu.PrefetchScalarGridSpec(
            num_scalar_prefetch=1,
            grid=(nb, 2 * n_ksteps),
            in_specs=[small(a) for a in extra_inputs] + [per_seq(qbd), per_seq(knew), per_seq(vnew), hbm, hbm],
            out_specs=pl.BlockSpec((None, dec, width), lambda b, j, pt: (b, 0, 0)),
            scratch_shapes=scratch,
        ),
        out_shape=jax.ShapeDtypeStruct((nb, dec, width), F32),
        compiler_params=pltpu.CompilerParams(
            dimension_semantics=("arbitrary", "arbitrary"), vmem_limit_bytes=VMEM_LIMIT),
        name=kern.func.__name__.strip("_"),
    )(page_table, *extra_inputs, qbd, knew, vnew, pool_k, pool_v)


def _block_diag_q(q, groups):
    b, dec, width = q.shape
    gd = width // groups
    t = q.reshape(b, dec, groups, gd).transpose(0, 2, 3, 1)
    eye = jnp.eye(groups, dtype=q.dtype)
    out = t[:, :, :, None, :] * eye[None, :, None, :, None]
    out = out.reshape(b, width, groups * dec)
    return jnp.pad(out, ((0, 0), (0, 0), (0, 128 - groups * dec)))


def _oproj_kernel(om_ref, od_ref, x_ref, w_ref, gpost_ref, gpre_ref, x1_ref, h2_ref):
    o = jnp.concatenate([om_ref[...].astype(BF16), od_ref[...].astype(BF16)], axis=1)
    x1 = x_ref[...] + _rms(_dot(o, w_ref[...]), gpost_ref[...])
    x1_ref[...] = x1
    h2_ref[...] = _rms(x1, gpre_ref[...]).astype(BF16)


def _oproj(om, od, x, w, gpost, gpre, tm):
    m, d = x.shape
    row = lambda width: pl.BlockSpec((tm, width), lambda i: (i, 0))
    const = lambda a: pl.BlockSpec(a.shape, lambda i: (0, 0))
    return pl.pallas_call(
        _oproj_kernel,
        grid=(m // tm,),
        in_specs=[row(om.shape[1]), row(od.shape[1]), row(d), const(w), const(gpost), const(gpre)],
        out_specs=[row(d), row(d)],
        out_shape=[jax.ShapeDtypeStruct((m, d), F32), jax.ShapeDtypeStruct((m, d), BF16)],
        compiler_params=pltpu.CompilerParams(
            dimension_semantics=("arbitrary",), vmem_limit_bytes=VMEM_LIMIT),
        name="oproj",
    )(om, od, x, w, gpost, gpre)


def _ffn_kernel(h_ref, x_ref, wg_ref, wu_ref, wd_ref, g_ref, o_ref, acc_ref):
    j = pl.program_id(1)

    @pl.when(j == 0)
    def _():
        acc_ref[...] = jnp.zeros(acc_ref.shape, F32)

    h = h_ref[...]
    gate = _dot(h, wg_ref[...])
    up = _dot(h, wu_ref[...])
    a = (gate * (1.0 / (1.0 + jnp.exp(-gate))) * up).astype(BF16)
    acc_ref[...] += _dot(a, wd_ref[...])

    @pl.when(j == pl.num_programs(1) - 1)
    def _():
        o_ref[...] = x_ref[...] + _rms(acc_ref[...], g_ref[...])


def _ffn(h, x, wg, wu, wd, g, tm, tf):
    m, d = x.shape
    f = wg.shape[1]
    return pl.pallas_call(
        _ffn_kernel,
        grid=(m // tm, f // tf),
        in_specs=[
            pl.BlockSpec((tm, d), lambda i, j: (i, 0)),
            pl.BlockSpec((tm, d), lambda i, j: (i, 0)),
            pl.BlockSpec((d, tf), lambda i, j: (0, j)),
            pl.BlockSpec((d, tf), lambda i, j: (0, j)),
            pl.BlockSpec((tf, d), lambda i, j: (j, 0)),
            pl.BlockSpec((1, d), lambda i, j: (0, 0)),
        ],
        out_specs=pl.BlockSpec((tm, d), lambda i, j: (i, 0)),
        out_shape=jax.ShapeDtypeStruct((m, d), F32),
        scratch_shapes=[pltpu.VMEM((tm, d), F32)],
        compiler_params=pltpu.CompilerParams(
            dimension_semantics=("arbitrary", "arbitrary"), vmem_limit_bytes=VMEM_LIMIT),
        name="ffn",
    )(h, x, wg, wu, wd, g)


def kernel(x_prompt, x_sample, cache_moba_k, cache_moba_v, cache_diff_k, cache_diff_v, page_table, g_pre_attn, w_in, lambda_q1, lambda_k1, lambda_q2, lambda_k2, g_subln, w_o, g_post_attn, g_pre_ffn, w_gate, w_up, w_down, g_post_ffn):
    batch, seq, d_model = x_prompt.shape
    dec_batch, dec_seq, _ = x_sample.shape
    depth, n_pool, page, h_moba, _ = cache_moba_k.shape
    h_diff = cache_diff_k.shape[3]
    n_heads = h_moba + h_diff
    wm, wd_ = h_moba * HEAD_DIM, h_diff * HEAD_DIM
    past_len = page_table.shape[1] * page
    assert seq % MOBA_BLOCK == 0 and past_len % MOBA_BLOCK == 0 and dec_seq <= MOBA_BLOCK
    assert wm == GROUP_WIDTH and wd_ == GROUP_WIDTH and h_moba * dec_seq <= 128 and 2 * h_diff * dec_seq <= 128
    assert cache_diff_k.shape[4] == HEAD_DIM and cache_diff_v.shape[4] == HEAD_DIM

    slopes = _alibi_slopes(n_heads)
    s_moba, s_diff = slopes[0::2], slopes[1::2]
    pad128 = lambda v: jnp.pad(v, (0, 128 - v.shape[0])).reshape(1, 128)
    lane_slope_m = pad128(jnp.repeat(s_moba, dec_seq))
    lane_slope_d = pad128(jnp.repeat(s_diff, 2 * dec_seq))
    pools = [c.reshape(depth, n_pool, -1, HEAD_DIM) for c in (cache_moba_k, cache_moba_v, cache_diff_k, cache_diff_v)]

    xp = x_prompt.reshape(batch * seq, d_model)
    xs = x_sample.reshape(dec_batch * dec_seq, d_model)
    tm_p, tm_s = 512, dec_batch * dec_seq
    kv_p = [[] for _ in range(4)]
    kv_s = [[] for _ in range(4)]
    for l in range(depth):
        lam_init = _lambda_init(l)
        row = lambda a: a[l].reshape(1, -1)
        lam_params = [row(lambda_q1), row(lambda_k1), row(lambda_q2), row(lambda_k2)]
        w_in_b, w_o_b = w_in[l].astype(BF16), w_o[l].astype(BF16)
        wg_b, wu_b, wd_b = w_gate[l].astype(BF16), w_up[l].astype(BF16), w_down[l].astype(BF16)

        mq, mk, mv, dq, dk, dv = _proj(xp, row(g_pre_attn), w_in_b, 1024, 512)
        seq3 = lambda a: a.reshape(batch, seq, -1)
        om = _prompt_call(_moba_prompt_kernel, [s_moba], seq3(mq), seq3(mk), seq3(mv), h_moba, "moba_prompt")
        od = _prompt_call(functools.partial(_diff_prompt_kernel, lam_init=lam_init),
                          [s_diff] + lam_params + [row(g_subln)], seq3(dq), seq3(dk), seq3(dv), h_diff,
                          "diff_prompt")
        x1, h2 = _oproj(om.reshape(batch * seq, wm), od.reshape(batch * seq, wd_), xp, w_o_b,
                        row(g_post_attn), row(g_pre_ffn), tm_p)
        xp = _ffn(h2, x1, wg_b, wu_b, wd_b, row(g_post_ffn), tm_p, 512)
        for dst, a in zip(kv_p, (mk, mv, dk, dv)):
            dst.append(a.reshape(batch, seq, -1, HEAD_DIM))

        mq, mk, mv, dq, dk, dv = _proj(xs, row(g_pre_attn), w_in_b, tm_s, GROUP_WIDTH)
        dec3 = lambda a: a.reshape(dec_batch, dec_seq, -1)
        om = _sample_attention(
            functools.partial(_moba_sample_kernel, past_len=past_len, n_heads=h_moba, layer=l),
            [lane_slope_m], _block_diag_q(dec3(mq), h_moba).astype(BF16), dec3(mk), dec3(mv),
            pools[0], pools[1], page_table, past_len // MOBA_BLOCK + 8)
        od = _sample_attention(
            functools.partial(_diff_sample_kernel, past_len=past_len, n_heads=h_diff, layer=l,
                              lam_init=lam_init),
            [lane_slope_d] + lam_params + [row(g_subln)],
            _block_diag_q(dec3(dq) * ((HEAD_DIM // 2) ** -0.5), 2 * h_diff).astype(BF16), dec3(dk), dec3(dv),
            pools[2], pools[3], page_table, 0)
        x1, h2 = _oproj(om.reshape(tm_s, wm), od.reshape(tm_s, wd_), xs, w_o_b,
                        row(g_post_attn), row(g_pre_ffn), tm_s)
        xs = _ffn(h2, x1, wg_b, wu_b, wd_b, row(g_post_ffn), tm_s, 512)
        for dst, a in zip(kv_s, (mk, mv, dk, dv)):
            dst.append(a.reshape(dec_batch, dec_seq, -1, HEAD_DIM))

    return (xp.reshape(batch, seq, d_model), xs.reshape(dec_batch, dec_seq, d_model),
            *[jnp.stack(t) for t in kv_p], *[jnp.stack(t) for t in kv_s])
```

```python
import functools
import math

import jax
import jax.numpy as jnp
from jax import lax
from jax.experimental import pallas as pl
from jax.experimental.pallas import tpu as pltpu

F32 = jnp.float32
BF16 = jnp.bfloat16
NEG_INF = float("-inf")

HEAD_DIM = 128
GROUP_WIDTH = 1024
LANES = 128
MOBA_BLOCK = 256
MOBA_TOPK = 3
RMS_EPS = 1e-6
PAGES_PER_STEP = 8
RING_SLOTS = 4
SOFTMAX_CHUNK = 3 * MOBA_BLOCK
VMEM_LIMIT = 56 * 1024 * 1024


def _alibi_slopes(n_heads):
    return 2.0 ** (-8.0 * jnp.arange(1, n_heads + 1, dtype=F32) / n_heads)


def _lambda_init(layer):
    return 0.8 - 0.6 * math.exp(-0.3 * layer)


def _rms(x, g):
    return (x * lax.rsqrt(jnp.mean(x * x, axis=-1, keepdims=True) + RMS_EPS)) * g


def _dot(a, b):
    return jnp.dot(a, b, preferred_element_type=F32)


def _pad_rows(a, rows):
    return jnp.concatenate([a, jnp.zeros((rows - a.shape[0], a.shape[1]), a.dtype)], axis=0)


def _proj_kernel(x_ref, g_ref, w_ref, *refs):
    outs, h_ref = refs[:-1], refs[-1]
    j = pl.program_id(1)

    @pl.when(j == 0)
    def _():
        h_ref[...] = _rms(x_ref[...], g_ref[...]).astype(BF16)

    per_out = GROUP_WIDTH // outs[0].shape[1]
    for k, o_ref in enumerate(outs):
        @pl.when((j >= k * per_out) & (j < (k + 1) * per_out))
        def _(o_ref=o_ref):
            o_ref[...] = _dot(h_ref[...], w_ref[...])


def _proj(x, g, w, tm, tn):
    m, d = x.shape
    n_out = w.shape[1] // GROUP_WIDTH
    per_out = GROUP_WIDTH // tn
    last_i = m // tm - 1

    def out_map(k):
        def index(i, j):
            done = j >= (k + 1) * per_out
            move = done & (i < last_i)
            col = jnp.where(done, jnp.where(move, 0, per_out - 1), jnp.clip(j - k * per_out, 0, per_out - 1))
            return jnp.where(move, i + 1, i), col
        return index

    return pl.pallas_call(
        _proj_kernel,
        grid=(m // tm, n_out * per_out),
        in_specs=[
            pl.BlockSpec((tm, d), lambda i, j: (i, 0)),
            pl.BlockSpec((1, d), lambda i, j: (0, 0)),
            pl.BlockSpec((d, tn), lambda i, j: (0, j)),
        ],
        out_specs=[pl.BlockSpec((tm, tn), out_map(k)) for k in range(n_out)],
        out_shape=[jax.ShapeDtypeStruct((m, GROUP_WIDTH), F32) for _ in range(n_out)],
        scratch_shapes=[pltpu.VMEM((tm, d), BF16)],
        compiler_params=pltpu.CompilerParams(
            dimension_semantics=("arbitrary", "arbitrary"), vmem_limit_bytes=VMEM_LIMIT),
        name="proj",
    )(x, g, w)


def _local_bias(slope):
    t = lax.broadcasted_iota(jnp.int32, (MOBA_BLOCK, MOBA_BLOCK), 1)
    j = lax.broadcasted_iota(jnp.int32, (MOBA_BLOCK, MOBA_BLOCK), 0)
    d = (t - j).astype(F32)
    b0 = -slope * d
    return b0, jnp.where(d >= 0, b0, NEG_INF)


def _stage_kv(k_ref, v_ref, kb_ref, vt_ref):
    blocks = []
    for n in range(k_ref.shape[0] // MOBA_BLOCK):
        rows = slice(n * MOBA_BLOCK, (n + 1) * MOBA_BLOCK)
        kblk = k_ref[rows, :]
        kb_ref[rows, :] = kblk.astype(BF16)
        vt_ref[:, rows] = jnp.transpose(v_ref[rows, :]).astype(BF16)
        blocks.append(kblk)
    return blocks


def _attend_block(i, qt, kb_ref, vt_ref, scale, b0, b0_diag, offsets):
    nk = (i + 1) * MOBA_BLOCK
    raw = _dot(kb_ref[0:nk, :], qt)
    ts, m = [], None
    for n in range(i + 1):
        t = raw[n * MOBA_BLOCK:(n + 1) * MOBA_BLOCK, :]
        if scale != 1.0:
            t = t * scale
        t = t + (b0_diag if n == i else b0)
        mb = jnp.max(t, axis=0, keepdims=True)
        if n < i:
            mb = mb + offsets[n]
        m = mb if m is None else jnp.maximum(m, mb)
        ts.append(t)
    ps, l = [], None
    for n in range(i + 1):
        p = jnp.exp(ts[n] - (m if n == i else m - offsets[n]))
        ps.append(p.astype(BF16))
        pl_sum = jnp.sum(p, axis=0, keepdims=True)
        l = pl_sum if l is None else l + pl_sum
    return _dot(vt_ref[:, 0:nk], jnp.concatenate(ps, axis=0)) / l


def _moba_prompt_blocks(slope, q_ref, k_ref, v_ref, o_ref, kb_ref, vt_ref):
    nb = q_ref.shape[0] // MOBA_BLOCK
    kblocks = _stage_kv(k_ref, v_ref, kb_ref, vt_ref)
    km = jnp.concatenate([jnp.mean(kb, axis=0, keepdims=True) for kb in kblocks], axis=0)
    km_hi = km.astype(BF16)
    km_lo = (km - km_hi.astype(F32)).astype(BF16)

    def block(i):
        b0, b0_diag = _local_bias(slope)
        blk = lax.broadcasted_iota(jnp.int32, (nb, MOBA_BLOCK), 0)
        rows = slice(i * MOBA_BLOCK, (i + 1) * MOBA_BLOCK)
        qt = jnp.transpose(q_ref[rows, :]).astype(BF16)
        offsets = []
        if i > 0:
            gate = _dot(km_hi, qt) + _dot(km_lo, qt)
            cnt = jnp.zeros(gate.shape, F32)
            for m in range(i):
                gm = gate[m:m + 1, :]
                beats = (gm > gate) | ((gm == gate) & (m < blk))
                cnt = cnt + jnp.where(beats, 1.0, 0.0)
            selb = jnp.where((blk < i) & (cnt < MOBA_TOPK), 0.0, NEG_INF)
            offsets = [selb[n:n + 1, :] - slope * float((i - n) * MOBA_BLOCK) for n in range(i)]
        o = _attend_block(i, qt, kb_ref, vt_ref, HEAD_DIM ** -0.5, b0, b0_diag, offsets)
        o_ref[rows, :] = jnp.transpose(o).astype(o_ref.dtype)

    return block


def _lambda(lam_refs, lam_init):
    lq1_ref, lk1_ref, lq2_ref, lk2_ref = lam_refs
    a = jnp.sum(lq1_ref[...] * lk1_ref[...], axis=-1, keepdims=True)
    b = jnp.sum(lq2_ref[...] * lk2_ref[...], axis=-1, keepdims=True)
    return jnp.exp(a) - jnp.exp(b) + lam_init


def _diff_prompt_blocks(slope, lam, lam_init, gsub_ref, q_ref, k_ref, v_ref, o_ref, kb_ref, vt_ref):
    qk_dim = HEAD_DIM // 2
    _stage_kv(k_ref, v_ref, kb_ref, vt_ref)

    def block(i):
        b0, b0_diag = _local_bias(slope)
        b0 = jnp.concatenate([b0, b0], axis=1)
        b0_diag = jnp.concatenate([b0_diag, b0_diag], axis=1)
        row = lax.broadcasted_iota(jnp.int32, (HEAD_DIM, MOBA_BLOCK), 0)
        rows = slice(i * MOBA_BLOCK, (i + 1) * MOBA_BLOCK)
        qt = jnp.transpose(q_ref[rows, :]) * (qk_dim ** -0.5)
        qt2 = jnp.concatenate([jnp.where(row < qk_dim, qt, 0.0), jnp.where(row >= qk_dim, qt, 0.0)],
                              axis=1).astype(BF16)
        offsets = [-slope * float((i - n) * MOBA_BLOCK) for n in range(i)]
        on = _attend_block(i, qt2, kb_ref, vt_ref, 1.0, b0, b0_diag, offsets)
        o = jnp.transpose(on[:, :MOBA_BLOCK] - lam * on[:, MOBA_BLOCK:])
        o_ref[rows, :] = (_rms(o, gsub_ref[...]) * (1.0 - lam_init)).astype(o_ref.dtype)

    return block


class _PageRing:
    def __init__(self, pt_ref, kpool, vpool, ring, sem, layer, n_heads):
        self.pt_ref, self.kpool, self.vpool, self.ring, self.sem = pt_ref, kpool, vpool, ring, sem
        self.layer, self.n_heads = layer, n_heads
        self.page_rows = kpool.shape[2]
        self.keys = self.page_rows // n_heads
        self.steps = pt_ref.shape[1] // PAGES_PER_STEP
        self.ahead = RING_SLOTS - 1
        assert self.steps % RING_SLOTS == 0 and self.ahead < self.steps
        self.seq = pl.program_id(0) // 2
        self.n_seq = pl.num_programs(0) // 2

    def _copy(self, pool, page_idx, slot, gi):
        return pltpu.make_async_copy(pool.at[self.layer, page_idx],
                                     self.ring.at[slot, pl.ds(gi * self.page_rows, self.page_rows)],
                                     self.sem.at[slot])

    def _start(self, seq, sweep, step):
        pool = self.vpool if sweep else self.kpool
        for gi in range(PAGES_PER_STEP):
            self._copy(pool, self.pt_ref[seq, step * PAGES_PER_STEP + gi], step % RING_SLOTS, gi).start()

    def prologue(self):
        for step in range(self.ahead):
            self._start(0, 0, step)

    def advance(self, sweep, step):
        later = sweep * self.steps + step + self.ahead
        if later < 2 * self.steps:
            self._start(self.seq, later // self.steps, later % self.steps)
        else:
            @pl.when(self.seq + 1 < self.n_seq)
            def _():
                self._start(self.seq + 1, 0, later - 2 * self.steps)
        for gi in range(PAGES_PER_STEP):
            self._copy(self.kpool, 0, step % RING_SLOTS, gi).wait()

    def pages(self, step):
        slot = step % RING_SLOTS
        return jnp.concatenate([
            jnp.concatenate(
                [self.ring[slot, pl.ds(gi * self.page_rows + h, self.keys, stride=self.n_heads), :].astype(BF16)
                 for h in range(self.n_heads)], axis=1)
            for gi in range(PAGES_PER_STEP)], axis=0)


def _block_diag_q(q, groups, qbd_ref):
    dec, width = q.shape
    gd = width // groups
    qt = jnp.transpose(_pad_rows(q, LANES))
    for gi in range(groups):
        rows = slice(gi * gd, (gi + 1) * gd)
        blk = qt[rows, :]
        if gi:
            blk = pltpu.roll(blk, gi * dec, axis=1)
        qbd_ref[rows, :] = blk.astype(BF16)


def _softmax_in_place(s_ref, rl_ref, logits_of):
    lanes = s_ref.shape[1]
    per_chunk = SOFTMAX_CHUNK // MOBA_BLOCK
    n_chunks = s_ref.shape[0] // SOFTMAX_CHUNK

    def pass1(c, m):
        for sub in range(per_chunk):
            blk = c * per_chunk + sub
            r0 = pl.multiple_of(blk * MOBA_BLOCK, MOBA_BLOCK)
            s = logits_of(blk, s_ref[pl.ds(r0, MOBA_BLOCK), :])
            s_ref[pl.ds(r0, MOBA_BLOCK), :] = s
            m = jnp.maximum(m, jnp.max(s, axis=0, keepdims=True))
        return m

    m = lax.fori_loop(0, n_chunks, pass1, jnp.full((1, lanes), NEG_INF, F32))

    def pass2(c, l):
        r0 = pl.multiple_of(c * SOFTMAX_CHUNK, SOFTMAX_CHUNK)
        p = jnp.exp(s_ref[pl.ds(r0, SOFTMAX_CHUNK), :] - m)
        s_ref[pl.ds(r0, SOFTMAX_CHUNK), :] = p
        return l + jnp.sum(p, axis=0, keepdims=True)

    l = lax.fori_loop(0, n_chunks, pass2, jnp.zeros((1, lanes), F32))
    rl_ref[...] = jnp.broadcast_to(1.0 / l, rl_ref.shape)


def _new_scores(knew_ref, qbd_ref, s_ref, past_len):
    s_ref[past_len:past_len + MOBA_BLOCK, :] = _dot(_pad_rows(knew_ref[...], MOBA_BLOCK).astype(BF16),
                                                    qbd_ref[...])


def _pv_step(rows, vb, s_ref, rl_ref, acc_ref):
    pt = jnp.transpose(s_ref[rows, :] * rl_ref[0:1, :]).astype(BF16)
    acc_ref[...] += _dot(pt, vb)


def _sweeps(ring, prompt_block, n_prompt_blocks, k_step, k_done, v_step, v_done):
    g = pl.program_id(0)

    @pl.when(g == 0)
    def _():
        ring.prologue()

    for sweep, step_fn, done_fn in ((0, k_step, k_done), (1, v_step, v_done)):
        @pl.when(g % 2 == sweep)
        def _(sweep=sweep, step_fn=step_fn, done_fn=done_fn):
            for r in range(max(ring.steps, n_prompt_blocks)):
                if r < ring.steps:
                    ring.advance(sweep, r)
                if r < n_prompt_blocks:
                    prompt_block(r)
                if r < ring.steps:
                    step_fn(r)
            done_fn()


def _moba_kernel(pt_ref, slopes_ref, lane_slope_ref, q_ref, k_ref, v_ref, qs_ref, knew_ref, vnew_ref,
                 kpool, vpool, o_ref, os_ref, kb_ref, vt_ref, s_ref, acc_ref, rl_ref, qbd_ref, ring_ref, sem,
                 selb_ref, *, layer, n_heads):
    ring = _PageRing(pt_ref, kpool, vpool, ring_ref, sem, layer, n_heads)
    past_len = ring.steps * PAGES_PER_STEP * ring.keys
    step_rows = PAGES_PER_STEP * ring.keys
    n_past_blk = past_len // MOBA_BLOCK
    dec = knew_ref.shape[0]
    scale = HEAD_DIM ** -0.5
    slope = slopes_ref[pl.program_id(0) % n_heads]
    prompt_block = _moba_prompt_blocks(slope, q_ref, k_ref, v_ref, o_ref, kb_ref, vt_ref)

    def k_step(r):
        if r == 0:
            _block_diag_q(qs_ref[...], n_heads, qbd_ref)
        sc = _dot(ring.pages(r), qbd_ref[...])
        s_ref[r * step_rows:(r + 1) * step_rows, :] = sc
        for b in range(step_rows // MOBA_BLOCK):
            selb_ref[r * (step_rows // MOBA_BLOCK) + b:r * (step_rows // MOBA_BLOCK) + b + 1, :] = \
                jnp.sum(sc[b * MOBA_BLOCK:(b + 1) * MOBA_BLOCK, :], axis=0, keepdims=True) * (1.0 / MOBA_BLOCK)

    def k_done():
        _new_scores(knew_ref, qbd_ref, s_ref, past_len)
        gate = selb_ref[0:n_past_blk, :]
        blk = lax.broadcasted_iota(jnp.int32, gate.shape, 0)
        cnt = jnp.zeros(gate.shape, F32)
        for m in range(n_past_blk):
            gm = gate[m:m + 1, :]
            beats = (gm > gate) | ((gm == gate) & (m < blk))
            cnt = cnt + jnp.where(beats, 1.0, 0.0)
        selb_ref[0:n_past_blk, :] = jnp.where(cnt < MOBA_TOPK, 0.0, NEG_INF)
        selb_ref[n_past_blk:n_past_blk + 8, :] = jnp.zeros((8, LANES), F32)

        lane = lax.broadcasted_iota(jnp.int32, (1, LANES), 1)
        q_pos = past_len + lane % dec
        lane_slope = lane_slope_ref[...]

        def logits_of(blk, raw):
            k_pos = blk * MOBA_BLOCK + lax.broadcasted_iota(jnp.int32, (MOBA_BLOCK, 1), 0)
            dist = q_pos - k_pos
            s = raw * scale - lane_slope * dist.astype(F32) + selb_ref[pl.ds(blk, 1), :]
            return jnp.where(dist >= 0, s, NEG_INF)

        _softmax_in_place(s_ref, rl_ref, logits_of)

    def v_step(r):
        if r == 0:
            acc_ref[...] = jnp.zeros(acc_ref.shape, F32)
        _pv_step(slice(r * step_rows, (r + 1) * step_rows), ring.pages(r), s_ref, rl_ref, acc_ref)

    def v_done():
        _pv_step(slice(past_len, past_len + MOBA_BLOCK), _pad_rows(vnew_ref[...], MOBA_BLOCK).astype(BF16),
                 s_ref, rl_ref, acc_ref)
        acc = acc_ref[...]
        for h in range(n_heads):
            cols = slice(h * HEAD_DIM, (h + 1) * HEAD_DIM)
            os_ref[:, cols] = acc[h * dec:(h + 1) * dec, cols]

    _sweeps(ring, prompt_block, q_ref.shape[0] // MOBA_BLOCK, k_step, k_done, v_step, v_done)


def _diff_kernel(pt_ref, slopes_ref, lane_slope_ref, lq1_ref, lk1_ref, lq2_ref, lk2_ref, gsub_ref,
                 q_ref, k_ref, v_ref, qs_ref, knew_ref, vnew_ref, kpool, vpool, o_ref, os_ref,
                 kb_ref, vt_ref, s_ref, acc_ref, rl_ref, qbd_ref, ring_ref, sem, *, layer, n_heads, lam_init):
    ring = _PageRing(pt_ref, kpool, vpool, ring_ref, sem, layer, n_heads)
    past_len = ring.steps * PAGES_PER_STEP * ring.keys
    step_rows = PAGES_PER_STEP * ring.keys
    dec = knew_ref.shape[0]
    qk_scale = (HEAD_DIM // 2) ** -0.5
    slope = slopes_ref[pl.program_id(0) % n_heads]
    lam = _lambda((lq1_ref, lk1_ref, lq2_ref, lk2_ref), lam_init)
    prompt_block = _diff_prompt_blocks(slope, lam, lam_init, gsub_ref, q_ref, k_ref, v_ref, o_ref, kb_ref, vt_ref)

    def k_step(r):
        if r == 0:
            _block_diag_q(qs_ref[...] * qk_scale, 2 * n_heads, qbd_ref)
        s_ref[r * step_rows:(r + 1) * step_rows, :] = _dot(ring.pages(r), qbd_ref[...])

    def k_done():
        _new_scores(knew_ref, qbd_ref, s_ref, past_len)
        lane = lax.broadcasted_iota(jnp.int32, (1, LANES), 1)
        q_pos = past_len + lane % dec
        lane_slope = lane_slope_ref[...]

        def logits_of(blk, raw):
            k_pos = blk * MOBA_BLOCK + lax.broadcasted_iota(jnp.int32, (MOBA_BLOCK, 1), 0)
            dist = q_pos - k_pos
            return jnp.where(dist >= 0, raw - lane_slope * dist.astype(F32), NEG_INF)

        _softmax_in_place(s_ref, rl_ref, logits_of)

    def v_step(r):
        if r == 0:
            acc_ref[...] = jnp.zeros(acc_ref.shape, F32)
        _pv_step(slice(r * step_rows, (r + 1) * step_rows), ring.pages(r), s_ref, rl_ref, acc_ref)

    def v_done():
        _pv_step(slice(past_len, past_len + MOBA_BLOCK), _pad_rows(vnew_ref[...], MOBA_BLOCK).astype(BF16),
                 s_ref, rl_ref, acc_ref)
        acc = acc_ref[...]
        for h in range(n_heads):
            cols = slice(h * HEAD_DIM, (h + 1) * HEAD_DIM)
            r0 = 2 * h * dec
            o = acc[r0:r0 + dec, cols] - lam * acc[r0 + dec:r0 + 2 * dec, cols]
            os_ref[:, cols] = _rms(o, gsub_ref[...]) * (1.0 - lam_init)

    _sweeps(ring, prompt_block, q_ref.shape[0] // MOBA_BLOCK, k_step, k_done, v_step, v_done)


def _attention(kern, name, slopes, small_inputs, q, k, v, qs, knew, vnew, pool_k, pool_v, page_table,
               n_heads, with_gate):
    b, t, width = q.shape
    n_seq, n_pages = page_table.shape
    page_rows, hd = pool_k.shape[2], pool_k.shape[3]
    dec = qs.shape[1]
    past_len = n_pages * (page_rows // n_heads)
    assert b * n_heads == 2 * n_seq, "one prompt (batch, head) and one sample sweep per grid step"
    prompt = pl.BlockSpec((None, t, HEAD_DIM), lambda g, pt: (g // n_heads, 0, g % n_heads))
    per_seq = pl.BlockSpec((None, dec, width), lambda g, pt: (g // 2, 0, 0))
    small = lambda a: pl.BlockSpec(a.shape, lambda g, pt: (0,) * a.ndim)
    hbm = pl.BlockSpec(memory_space=pl.ANY)
    scratch = [
        pltpu.VMEM((t, HEAD_DIM), BF16),
        pltpu.VMEM((HEAD_DIM, t), BF16),
        pltpu.VMEM((past_len + MOBA_BLOCK, LANES), F32),
        pltpu.VMEM((LANES, width), F32),
        pltpu.VMEM((8, LANES), F32),
        pltpu.VMEM((width, LANES), BF16),
        pltpu.VMEM((RING_SLOTS, PAGES_PER_STEP * page_rows, hd), pool_k.dtype),
        pltpu.SemaphoreType.DMA((RING_SLOTS,)),
    ]
    if with_gate:
        scratch.append(pltpu.VMEM((past_len // MOBA_BLOCK + 8, LANES), F32))
    return pl.pallas_call(
        kern,
        grid_spec=pltpu.PrefetchScalarGridSpec(
            num_scalar_prefetch=1,
            grid=(b * n_heads,),
            in_specs=[pl.BlockSpec(memory_space=pltpu.SMEM)] + [small(a) for a in small_inputs]
                     + [prompt, prompt, prompt, per_seq, per_seq, per_seq, hbm, hbm],
            out_specs=[prompt, per_seq],
            scratch_shapes=scratch,
        ),
        out_shape=[jax.ShapeDtypeStruct((b, t, width), BF16), jax.ShapeDtypeStruct((n_seq, dec, width), F32)],
        compiler_params=pltpu.CompilerParams(
            dimension_semantics=("arbitrary",), vmem_limit_bytes=VMEM_LIMIT),
        name=name,
    )(page_table, slopes, *small_inputs, q, k, v, qs, knew, vnew, pool_k, pool_v)


def _oproj_kernel(om_ref, od_ref, x_ref, w_ref, gpost_ref, gpre_ref, x1_ref, h2_ref):
    o = jnp.concatenate([om_ref[...].astype(BF16), od_ref[...].astype(BF16)], axis=1)
    x1 = x_ref[...] + _rms(_dot(o, w_ref[...]), gpost_ref[...])
    x1_ref[...] = x1
    h2_ref[...] = _rms(x1, gpre_ref[...]).astype(BF16)


def _oproj(om, od, x, w, gpost, gpre, tm):
    m, d = x.shape
    row = lambda width: pl.BlockSpec((tm, width), lambda i: (i, 0))
    const = lambda a: pl.BlockSpec(a.shape, lambda i: (0, 0))
    return pl.pallas_call(
        _oproj_kernel,
        grid=(m // tm,),
        in_specs=[row(om.shape[1]), row(od.shape[1]), row(d), const(w), const(gpost), const(gpre)],
        out_specs=[row(d), row(d)],
        out_shape=[jax.ShapeDtypeStruct((m, d), F32), jax.ShapeDtypeStruct((m, d), BF16)],
        compiler_params=pltpu.CompilerParams(
            dimension_semantics=("arbitrary",), vmem_limit_bytes=VMEM_LIMIT),
        name="oproj",
    )(om, od, x, w, gpost, gpre)


def _ffn_kernel(h_ref, x_ref, wg_ref, wu_ref, wd_ref, g_ref, o_ref, acc_ref):
    j = pl.program_id(1)

    @pl.when(j == 0)
    def _():
        acc_ref[...] = jnp.zeros(acc_ref.shape, F32)

    h = h_ref[...]
    gate = _dot(h, wg_ref[...])
    up = _dot(h, wu_ref[...])
    a = (gate * (1.0 / (1.0 + jnp.exp(-gate))) * up).astype(BF16)
    acc_ref[...] += _dot(a, wd_ref[...])

    @pl.when(j == pl.num_programs(1) - 1)
    def _():
        o_ref[...] = x_ref[...] + _rms(acc_ref[...], g_ref[...])


def _ffn(h, x, wg, wu, wd, g, tm, tf):
    m, d = x.shape
    f = wg.shape[1]
    return pl.pallas_call(
        _ffn_kernel,
        grid=(m // tm, f // tf),
        in_specs=[
            pl.BlockSpec((tm, d), lambda i, j: (i, 0)),
            pl.BlockSpec((tm, d), lambda i, j: (i, 0)),
            pl.BlockSpec((d, tf), lambda i, j: (0, j)),
            pl.BlockSpec((d, tf), lambda i, j: (0, j)),
            pl.BlockSpec((tf, d), lambda i, j: (j, 0)),
            pl.BlockSpec((1, d), lambda i, j: (0, 0)),
        ],
        out_specs=pl.BlockSpec((tm, d), lambda i, j: (i, 0)),
        out_shape=jax.ShapeDtypeStruct((m, d), F32),
        scratch_shapes=[pltpu.VMEM((tm, d), F32)],
        compiler_params=pltpu.CompilerParams(
            dimension_semantics=("arbitrary", "arbitrary"), vmem_limit_bytes=VMEM_LIMIT),
        name="ffn",
    )(h, x, wg, wu, wd, g)


def kernel(x_prompt, x_sample, cache_moba_k, cache_moba_v, cache_diff_k, cache_diff_v, page_table, g_pre_attn, w_in, lambda_q1, lambda_k1, lambda_q2, lambda_k2, g_subln, w_o, g_post_attn, g_pre_ffn, w_gate, w_up, w_down, g_post_ffn):
    batch, seq, d_model = x_prompt.shape
    dec_batch, dec_seq, _ = x_sample.shape
    depth, n_pool, page, h_moba, _ = cache_moba_k.shape
    h_diff = cache_diff_k.shape[3]
    n_heads = h_moba + h_diff
    wm, wd_ = h_moba * HEAD_DIM, h_diff * HEAD_DIM
    past_len = page_table.shape[1] * page
    assert seq % MOBA_BLOCK == 0 and (past_len + MOBA_BLOCK) % SOFTMAX_CHUNK == 0
    assert dec_seq <= MOBA_BLOCK and (page * PAGES_PER_STEP) % MOBA_BLOCK == 0
    assert wm == GROUP_WIDTH and wd_ == GROUP_WIDTH and h_moba * dec_seq <= LANES and 2 * h_diff * dec_seq <= LANES
    assert cache_diff_k.shape[4] == HEAD_DIM and cache_diff_v.shape[4] == HEAD_DIM

    slopes = _alibi_slopes(n_heads)
    s_moba, s_diff = slopes[0::2], slopes[1::2]
    pad_lanes = lambda v: jnp.pad(v, (0, LANES - v.shape[0])).reshape(1, LANES)
    lane_slope_m = pad_lanes(jnp.repeat(s_moba, dec_seq))
    lane_slope_d = pad_lanes(jnp.repeat(s_diff, 2 * dec_seq))
    pools = [c.reshape(depth, n_pool, -1, HEAD_DIM) for c in (cache_moba_k, cache_moba_v, cache_diff_k, cache_diff_v)]

    xp = x_prompt.reshape(batch * seq, d_model)
    xs = x_sample.reshape(dec_batch * dec_seq, d_model)
    tm_p, tm_s = 512, dec_batch * dec_seq
    kv_p = [[] for _ in range(4)]
    kv_s = [[] for _ in range(4)]
    for l in range(depth):
        lam_init = _lambda_init(l)
        row = lambda a: a[l].reshape(1, -1)
        lam_params = [row(lambda_q1), row(lambda_k1), row(lambda_q2), row(lambda_k2)]
        w_in_b, w_o_b = w_in[l].astype(BF16), w_o[l].astype(BF16)
        wg_b, wu_b, wd_b = w_gate[l].astype(BF16), w_up[l].astype(BF16), w_down[l].astype(BF16)

        proj_p = _proj(xp, row(g_pre_attn), w_in_b, 1024, 512)
        proj_s = _proj(xs, row(g_pre_attn), w_in_b, tm_s, GROUP_WIDTH)
        seq3 = lambda a: a.reshape(batch, seq, -1)
        dec3 = lambda a: a.reshape(dec_batch, dec_seq, -1)
        mq, mk, mv, dq, dk, dv = [seq3(a) for a in proj_p]
        mqs, mks, mvs, dqs, dks, dvs = [dec3(a) for a in proj_s]
        om_p, om_s = _attention(
            functools.partial(_moba_kernel, layer=l, n_heads=h_moba), "moba_attention", s_moba,
            [lane_slope_m], mq, mk, mv, mqs, mks, mvs, pools[0], pools[1], page_table, h_moba, True)
        od_p, od_s = _attention(
            functools.partial(_diff_kernel, layer=l, n_heads=h_diff, lam_init=lam_init), "diff_attention", s_diff,
            [lane_slope_d] + lam_params + [row(g_subln)], dq, dk, dv, dqs, dks, dvs, pools[2], pools[3],
            page_table, h_diff, False)

        x1, h2 = _oproj(om_p.reshape(batch * seq, wm), od_p.reshape(batch * seq, wd_), xp, w_o_b,
                        row(g_post_attn), row(g_pre_ffn), tm_p)
        xp = _ffn(h2, x1, wg_b, wu_b, wd_b, row(g_post_ffn), tm_p, 512)
        x1, h2 = _oproj(om_s.reshape(tm_s, wm), od_s.reshape(tm_s, wd_), xs, w_o_b,
                        row(g_post_attn), row(g_pre_ffn), tm_s)
        xs = _ffn(h2, x1, wg_b, wu_b, wd_b, row(g_post_ffn), tm_s, 512)
        for dst, a in zip(kv_p, (mk, mv, dk, dv)):
            dst.append(a.reshape(batch, seq, -1, HEAD_DIM))
        for dst, a in zip(kv_s, (mks, mvs, dks, dvs)):
            dst.append(a.reshape(dec_batch, dec_seq, -1, HEAD_DIM))

    return (xp.reshape(batch, seq, d_model), xs.reshape(dec_batch, dec_seq, d_model),
            *[jnp.stack(t) for t in kv_p], *[jnp.stack(t) for t in kv_s])
```

```python
import functools
import math

import jax
import jax.numpy as jnp
from jax import lax
from jax.experimental import pallas as pl
from jax.experimental.pallas import tpu as pltpu

F32 = jnp.float32
BF16 = jnp.bfloat16
NEG_INF = float("-inf")

HEAD_DIM = 128
GROUP_WIDTH = 1024
LANES = 128
MOBA_BLOCK = 256
MOBA_TOPK = 3
RMS_EPS = 1e-6
PAGES_PER_STEP = 8
RING_SLOTS = 4
VMEM_LIMIT = 56 * 1024 * 1024


def _alibi_slopes(n_heads):
    return 2.0 ** (-8.0 * jnp.arange(1, n_heads + 1, dtype=F32) / n_heads)


def _lambda_init(layer):
    return 0.8 - 0.6 * math.exp(-0.3 * layer)


def _rms(x, g):
    return (x * lax.rsqrt(jnp.mean(x * x, axis=-1, keepdims=True) + RMS_EPS)) * g


def _dot(a, b):
    return jnp.dot(a, b, preferred_element_type=F32)


def _dot_row_slabs(a, b, slabs=2):
    rows = a.shape[0] // slabs
    return jnp.concatenate([_dot(a[i * rows:(i + 1) * rows, :], b) for i in range(slabs)], axis=0)


def _pad_rows(a, rows):
    return jnp.concatenate([a, jnp.zeros((rows - a.shape[0], a.shape[1]), a.dtype)], axis=0)


def _proj_kernel(x_ref, g_ref, w_ref, *refs):
    outs, h_ref = refs[:-1], refs[-1]
    j = pl.program_id(1)

    @pl.when(j == 0)
    def _():
        h_ref[...] = _rms(x_ref[...], g_ref[...]).astype(BF16)

    per_out = GROUP_WIDTH // outs[0].shape[1]
    for k, o_ref in enumerate(outs):
        @pl.when((j >= k * per_out) & (j < (k + 1) * per_out))
        def _(o_ref=o_ref):
            o_ref[...] = _dot(h_ref[...], w_ref[...])


def _proj(x, g, w, tm, tn):
    m, d = x.shape
    n_out = w.shape[1] // GROUP_WIDTH
    per_out = GROUP_WIDTH // tn
    last_i = m // tm - 1

    def out_map(k):
        def index(i, j):
            done = j >= (k + 1) * per_out
            move = done & (i < last_i)
            col = jnp.where(done, jnp.where(move, 0, per_out - 1), jnp.clip(j - k * per_out, 0, per_out - 1))
            return jnp.where(move, i + 1, i), col
        return index

    return pl.pallas_call(
        _proj_kernel,
        grid=(m // tm, n_out * per_out),
        in_specs=[
            pl.BlockSpec((tm, d), lambda i, j: (jnp.minimum(i + jnp.minimum(j, 1), last_i), 0)),
            pl.BlockSpec((1, d), lambda i, j: (0, 0)),
            pl.BlockSpec((d, tn), lambda i, j: (0, j)),
        ],
        out_specs=[pl.BlockSpec((tm, tn), out_map(k)) for k in range(n_out)],
        out_shape=[jax.ShapeDtypeStruct((m, GROUP_WIDTH), F32) for _ in range(n_out)],
        scratch_shapes=[pltpu.VMEM((tm, d), BF16)],
        compiler_params=pltpu.CompilerParams(
            dimension_semantics=("arbitrary", "arbitrary"), vmem_limit_bytes=VMEM_LIMIT),
        name="proj",
    )(x, g, w)


def _local_bias(slope):
    t = lax.broadcasted_iota(jnp.int32, (MOBA_BLOCK, MOBA_BLOCK), 1)
    j = lax.broadcasted_iota(jnp.int32, (MOBA_BLOCK, MOBA_BLOCK), 0)
    d = (t - j).astype(F32)
    b0 = -slope * d
    return b0, jnp.where(d >= 0, b0, NEG_INF)


def _stage_kv(k_ref, v_ref, kb_ref, vt_ref):
    blocks = []
    for n in range(k_ref.shape[0] // MOBA_BLOCK):
        rows = slice(n * MOBA_BLOCK, (n + 1) * MOBA_BLOCK)
        kblk = k_ref[rows, :]
        kb_ref[rows, :] = kblk.astype(BF16)
        vt_ref[:, rows] = jnp.transpose(v_ref[rows, :]).astype(BF16)
        blocks.append(kblk)
    return blocks


def _attend_block(i, qt, kb_ref, vt_ref, scale, b0, b0_diag, offsets):
    nk = (i + 1) * MOBA_BLOCK
    raw = _dot(kb_ref[0:nk, :], qt)
    ts, m = [], None
    for n in range(i + 1):
        t = raw[n * MOBA_BLOCK:(n + 1) * MOBA_BLOCK, :]
        if scale != 1.0:
            t = t * scale
        t = t + (b0_diag if n == i else b0)
        mb = jnp.max(t, axis=0, keepdims=True)
        if n < i:
            mb = mb + offsets[n]
        m = mb if m is None else jnp.maximum(m, mb)
        ts.append(t)
    ps, l = [], None
    for n in range(i + 1):
        p = jnp.exp(ts[n] - (m if n == i else m - offsets[n]))
        ps.append(p.astype(BF16))
        pl_sum = jnp.sum(p, axis=0, keepdims=True)
        l = pl_sum if l is None else l + pl_sum
    return _dot(vt_ref[:, 0:nk], jnp.concatenate(ps, axis=0)) / l


def _moba_prompt_blocks(slope, q_ref, k_ref, v_ref, o_ref, kb_ref, vt_ref):
    nb = q_ref.shape[0] // MOBA_BLOCK
    kblocks = _stage_kv(k_ref, v_ref, kb_ref, vt_ref)
    km = jnp.concatenate([jnp.mean(kb, axis=0, keepdims=True) for kb in kblocks], axis=0)
    km_hi = km.astype(BF16)
    km_lo = (km - km_hi.astype(F32)).astype(BF16)

    def block(i):
        b0, b0_diag = _local_bias(slope)
        blk = lax.broadcasted_iota(jnp.int32, (nb, MOBA_BLOCK), 0)
        rows = slice(i * MOBA_BLOCK, (i + 1) * MOBA_BLOCK)
        qt = jnp.transpose(q_ref[rows, :]).astype(BF16)
        offsets = []
        if i > 0:
            gate = _dot(km_hi, qt) + _dot(km_lo, qt)
            cnt = jnp.zeros(gate.shape, F32)
            for m in range(i):
                gm = gate[m:m + 1, :]
                beats = (gm > gate) | ((gm == gate) & (m < blk))
                cnt = cnt + jnp.where(beats, 1.0, 0.0)
            selb = jnp.where((blk < i) & (cnt < MOBA_TOPK), 0.0, NEG_INF)
            offsets = [selb[n:n + 1, :] - slope * float((i - n) * MOBA_BLOCK) for n in range(i)]
        o = _attend_block(i, qt, kb_ref, vt_ref, HEAD_DIM ** -0.5, b0, b0_diag, offsets)
        o_ref[rows, :] = jnp.transpose(o).astype(o_ref.dtype)

    return block


def _lambda(lam_refs, lam_init):
    lq1_ref, lk1_ref, lq2_ref, lk2_ref = lam_refs
    a = jnp.sum(lq1_ref[...] * lk1_ref[...], axis=-1, keepdims=True)
    b = jnp.sum(lq2_ref[...] * lk2_ref[...], axis=-1, keepdims=True)
    return jnp.exp(a) - jnp.exp(b) + lam_init


def _diff_prompt_blocks(slope, lam, lam_init, gsub_ref, q_ref, k_ref, v_ref, o_ref, kb_ref, vt_ref):
    qk_dim = HEAD_DIM // 2
    _stage_kv(k_ref, v_ref, kb_ref, vt_ref)

    def block(i):
        b0, b0_diag = _local_bias(slope)
        b0 = jnp.concatenate([b0, b0], axis=1)
        b0_diag = jnp.concatenate([b0_diag, b0_diag], axis=1)
        row = lax.broadcasted_iota(jnp.int32, (HEAD_DIM, MOBA_BLOCK), 0)
        rows = slice(i * MOBA_BLOCK, (i + 1) * MOBA_BLOCK)
        qt = jnp.transpose(q_ref[rows, :]) * (qk_dim ** -0.5)
        qt2 = jnp.concatenate([jnp.where(row < qk_dim, qt, 0.0), jnp.where(row >= qk_dim, qt, 0.0)],
                              axis=1).astype(BF16)
        offsets = [-slope * float((i - n) * MOBA_BLOCK) for n in range(i)]
        on = _attend_block(i, qt2, kb_ref, vt_ref, 1.0, b0, b0_diag, offsets)
        o = jnp.transpose(on[:, :MOBA_BLOCK] - lam * on[:, MOBA_BLOCK:])
        o_ref[rows, :] = (_rms(o, gsub_ref[...]) * (1.0 - lam_init)).astype(o_ref.dtype)

    return block


class _PageRing:
    def __init__(self, pt_ref, kpool, vpool, ring, sem, layer, n_heads):
        self.pt_ref, self.kpool, self.vpool, self.ring, self.sem = pt_ref, kpool, vpool, ring, sem
        self.layer, self.n_heads = layer, n_heads
        self.page_rows = kpool.shape[2]
        self.keys = self.page_rows // n_heads
        self.steps = pt_ref.shape[1] // PAGES_PER_STEP
        self.ahead = RING_SLOTS - 1
        assert self.steps % RING_SLOTS == 0 and self.ahead < self.steps
        self.seq = pl.program_id(0) // 2
        self.n_seq = pl.num_programs(0) // 2

    def _copy(self, pool, page_idx, slot, gi):
        return pltpu.make_async_copy(pool.at[self.layer, page_idx],
                                     self.ring.at[slot, pl.ds(gi * self.page_rows, self.page_rows)],
                                     self.sem.at[slot])

    def _start(self, seq, sweep, step):
        pool = self.vpool if sweep else self.kpool
        for gi in range(PAGES_PER_STEP):
            self._copy(pool, self.pt_ref[seq, step * PAGES_PER_STEP + gi], step % RING_SLOTS, gi).start()

    def prologue(self):
        for step in range(self.ahead):
            self._start(0, 0, step)

    def advance(self, sweep, step):
        later = sweep * self.steps + step + self.ahead
        if later < 2 * self.steps:
            self._start(self.seq, later // self.steps, later % self.steps)
        else:
            @pl.when(self.seq + 1 < self.n_seq)
            def _():
                self._start(self.seq + 1, 0, later - 2 * self.steps)
        for gi in range(PAGES_PER_STEP):
            self._copy(self.kpool, 0, step % RING_SLOTS, gi).wait()

    def pages(self, step):
        slot = step % RING_SLOTS
        return jnp.concatenate([
            jnp.concatenate(
                [self.ring[slot, pl.ds(gi * self.page_rows + h, self.keys, stride=self.n_heads), :].astype(BF16)
                 for h in range(self.n_heads)], axis=1)
            for gi in range(PAGES_PER_STEP)], axis=0)


def _block_diag_q(q, groups, qbd_ref):
    dec, width = q.shape
    gd = width // groups
    qt = jnp.transpose(_pad_rows(q, LANES))
    for gi in range(groups):
        rows = slice(gi * gd, (gi + 1) * gd)
        blk = qt[rows, :]
        if gi:
            blk = pltpu.roll(blk, gi * dec, axis=1)
        qbd_ref[rows, :] = blk.astype(BF16)


class _SampleScores:
    def __init__(self, s_ref, stat_ref, l_ref, acc_ref, qbd_ref, lane_slope_ref, scale, past_len, dec):
        self.s_ref, self.stat_ref, self.l_ref, self.acc_ref, self.qbd_ref = s_ref, stat_ref, l_ref, acc_ref, qbd_ref
        self.scale, self.past_len = scale, past_len
        self.n_past_blk = past_len // MOBA_BLOCK
        self.lane_slope = lane_slope_ref[...]
        self.q_pos = past_len + lax.broadcasted_iota(jnp.int32, (1, LANES), 1) % dec

    def _rows(self, blk):
        return slice(blk * MOBA_BLOCK, (blk + 1) * MOBA_BLOCK)

    def _store(self, blk, logits):
        self.stat_ref[blk:blk + 1, :] = jnp.max(logits, axis=0, keepdims=True)
        self.s_ref[self._rows(blk), :] = logits

    def store_cached(self, blk, raw):
        in_block = lax.broadcasted_iota(jnp.int32, (MOBA_BLOCK, LANES), 0).astype(F32)
        first = -self.lane_slope * (self.q_pos - blk * MOBA_BLOCK).astype(F32)
        scaled = raw if self.scale == 1.0 else raw * self.scale
        self._store(blk, scaled + self.lane_slope * in_block + first)

    def store_new(self, knew):
        raw = _dot_row_slabs(_pad_rows(knew, MOBA_BLOCK).astype(BF16), self.qbd_ref[...])
        k_pos = self.past_len + lax.broadcasted_iota(jnp.int32, (MOBA_BLOCK, 1), 0)
        dist = self.q_pos - k_pos
        scaled = raw if self.scale == 1.0 else raw * self.scale
        self._store(self.n_past_blk, jnp.where(dist >= 0, scaled - self.lane_slope * dist.astype(F32), NEG_INF))

    def finish(self, selb=None):
        n = self.n_past_blk
        past = self.stat_ref[0:n, :] if selb is None else self.stat_ref[0:n, :] + selb
        m = jnp.maximum(jnp.max(past, axis=0, keepdims=True), self.stat_ref[n:n + 1, :])
        self.stat_ref[0:n, :] = jnp.broadcast_to(-m, past.shape) if selb is None else selb - m
        self.stat_ref[n:n + 1, :] = -m

    def pv(self, blocks, vb, first):
        l = jnp.zeros((1, LANES), F32) if first else self.l_ref[0:1, :]
        ps = []
        for blk in blocks:
            p = jnp.exp(self.s_ref[self._rows(blk), :] + self.stat_ref[blk:blk + 1, :])
            l = l + jnp.sum(p, axis=0, keepdims=True)
            ps.append(p)
        self.l_ref[0:1, :] = l
        part = _dot(jnp.transpose(jnp.concatenate(ps, axis=0)).astype(BF16), vb)
        self.acc_ref[...] = part if first else self.acc_ref[...] + part

    def row_scale(self):
        return jnp.transpose(jnp.broadcast_to(1.0 / self.l_ref[0:1, :], (LANES, LANES)))


def _sweeps(ring, prompt_block, n_prompt_blocks, k_step, k_done, v_step, v_done):
    g = pl.program_id(0)

    @pl.when(g == 0)
    def _():
        ring.prologue()

    for sweep, step_fn, done_fn in ((0, k_step, k_done), (1, v_step, v_done)):
        @pl.when(g % 2 == sweep)
        def _(sweep=sweep, step_fn=step_fn, done_fn=done_fn):
            for r in range(max(ring.steps, n_prompt_blocks)):
                if r < ring.steps:
                    ring.advance(sweep, r)
                if r < n_prompt_blocks:
                    prompt_block(r)
                if r < ring.steps:
                    step_fn(r)
            done_fn()


def _moba_kernel(pt_ref, slopes_ref, lane_slope_ref, q_ref, k_ref, v_ref, qs_ref, knew_ref, vnew_ref,
                 kpool, vpool, o_ref, os_ref, kb_ref, vt_ref, s_ref, acc_ref, l_ref, qbd_ref, ring_ref, sem,
                 stat_ref, gate_ref, *, layer, n_heads):
    ring = _PageRing(pt_ref, kpool, vpool, ring_ref, sem, layer, n_heads)
    past_len = ring.steps * PAGES_PER_STEP * ring.keys
    blocks_per_step = PAGES_PER_STEP * ring.keys // MOBA_BLOCK
    n_past_blk = past_len // MOBA_BLOCK
    dec = knew_ref.shape[0]
    slope = slopes_ref[pl.program_id(0) % n_heads]
    prompt_block = _moba_prompt_blocks(slope, q_ref, k_ref, v_ref, o_ref, kb_ref, vt_ref)
    scores = _SampleScores(s_ref, stat_ref, l_ref, acc_ref, qbd_ref, lane_slope_ref, HEAD_DIM ** -0.5,
                           past_len, dec)

    def k_step(r):
        if r == 0:
            _block_diag_q(qs_ref[...], n_heads, qbd_ref)
        sc = _dot_row_slabs(ring.pages(r), qbd_ref[...])
        for b in range(blocks_per_step):
            blk = r * blocks_per_step + b
            raw = sc[b * MOBA_BLOCK:(b + 1) * MOBA_BLOCK, :]
            gate_ref[blk:blk + 1, :] = jnp.sum(raw, axis=0, keepdims=True) * (1.0 / MOBA_BLOCK)
            scores.store_cached(blk, raw)

    def k_done():
        scores.store_new(knew_ref[...])
        gate = gate_ref[0:n_past_blk, :]
        blk = lax.broadcasted_iota(jnp.int32, gate.shape, 0)
        cnt = jnp.zeros(gate.shape, F32)
        for m in range(n_past_blk):
            gm = gate[m:m + 1, :]
            beats = (gm > gate) | ((gm == gate) & (m < blk))
            cnt = cnt + jnp.where(beats, 1.0, 0.0)
        scores.finish(jnp.where(cnt < MOBA_TOPK, 0.0, NEG_INF))

    def v_step(r):
        scores.pv(range(r * blocks_per_step, (r + 1) * blocks_per_step), ring.pages(r), r == 0)

    def v_done():
        scores.pv([n_past_blk], _pad_rows(vnew_ref[...], MOBA_BLOCK).astype(BF16), False)
        acc, inv_l = acc_ref[...], scores.row_scale()
        for h in range(n_heads):
            cols, lanes = slice(h * HEAD_DIM, (h + 1) * HEAD_DIM), slice(h * dec, (h + 1) * dec)
            os_ref[:, cols] = acc[lanes, cols] * inv_l[lanes, :]

    _sweeps(ring, prompt_block, q_ref.shape[0] // MOBA_BLOCK, k_step, k_done, v_step, v_done)


def _diff_kernel(pt_ref, slopes_ref, lane_slope_ref, lq1_ref, lk1_ref, lq2_ref, lk2_ref, gsub_ref,
                 q_ref, k_ref, v_ref, qs_ref, knew_ref, vnew_ref, kpool, vpool, o_ref, os_ref,
                 kb_ref, vt_ref, s_ref, acc_ref, l_ref, qbd_ref, ring_ref, sem, stat_ref,
                 *, layer, n_heads, lam_init):
    ring = _PageRing(pt_ref, kpool, vpool, ring_ref, sem, layer, n_heads)
    past_len = ring.steps * PAGES_PER_STEP * ring.keys
    blocks_per_step = PAGES_PER_STEP * ring.keys // MOBA_BLOCK
    dec = knew_ref.shape[0]
    qk_scale = (HEAD_DIM // 2) ** -0.5
    slope = slopes_ref[pl.program_id(0) % n_heads]
    lam = _lambda((lq1_ref, lk1_ref, lq2_ref, lk2_ref), lam_init)
    prompt_block = _diff_prompt_blocks(slope, lam, lam_init, gsub_ref, q_ref, k_ref, v_ref, o_ref, kb_ref, vt_ref)
    scores = _SampleScores(s_ref, stat_ref, l_ref, acc_ref, qbd_ref, lane_slope_ref, 1.0, past_len, dec)

    def k_step(r):
        if r == 0:
            _block_diag_q(qs_ref[...] * qk_scale, 2 * n_heads, qbd_ref)
        sc = _dot_row_slabs(ring.pages(r), qbd_ref[...])
        for b in range(blocks_per_step):
            scores.store_cached(r * blocks_per_step + b, sc[b * MOBA_BLOCK:(b + 1) * MOBA_BLOCK, :])

    def k_done():
        scores.store_new(knew_ref[...])
        scores.finish()

    def v_step(r):
        scores.pv(range(r * blocks_per_step, (r + 1) * blocks_per_step), ring.pages(r), r == 0)

    def v_done():
        scores.pv([scores.n_past_blk], _pad_rows(vnew_ref[...], MOBA_BLOCK).astype(BF16), False)
        acc, inv_l = acc_ref[...], scores.row_scale()
        for h in range(n_heads):
            cols = slice(h * HEAD_DIM, (h + 1) * HEAD_DIM)
            map1, map2 = slice(2 * h * dec, (2 * h + 1) * dec), slice((2 * h + 1) * dec, (2 * h + 2) * dec)
            o = acc[map1, cols] * inv_l[map1, :] - lam * (acc[map2, cols] * inv_l[map2, :])
            os_ref[:, cols] = _rms(o, gsub_ref[...]) * (1.0 - lam_init)

    _sweeps(ring, prompt_block, q_ref.shape[0] // MOBA_BLOCK, k_step, k_done, v_step, v_done)


def _attention(kern, name, slopes, small_inputs, q, k, v, qs, knew, vnew, pool_k, pool_v, page_table,
               n_heads, with_gate):
    b, t, width = q.shape
    n_seq, n_pages = page_table.shape
    page_rows, hd = pool_k.shape[2], pool_k.shape[3]
    dec = qs.shape[1]
    past_len = n_pages * (page_rows // n_heads)
    assert b * n_heads == 2 * n_seq, "one prompt (batch, head) and one sample sweep per grid step"
    prompt = pl.BlockSpec((None, t, HEAD_DIM), lambda g, pt: (g // n_heads, 0, g % n_heads))
    per_seq = pl.BlockSpec((None, dec, width), lambda g, pt: (g // 2, 0, 0))
    small = lambda a: pl.BlockSpec(a.shape, lambda g, pt: (0,) * a.ndim)
    hbm = pl.BlockSpec(memory_space=pl.ANY)
    scratch = [
        pltpu.VMEM((t, HEAD_DIM), BF16),
        pltpu.VMEM((HEAD_DIM, t), BF16),
        pltpu.VMEM((past_len + MOBA_BLOCK, LANES), F32),
        pltpu.VMEM((LANES, width), F32),
        pltpu.VMEM((8, LANES), F32),
        pltpu.VMEM((width, LANES), BF16),
        pltpu.VMEM((RING_SLOTS, PAGES_PER_STEP * page_rows, hd), pool_k.dtype),
        pltpu.SemaphoreType.DMA((RING_SLOTS,)),
        pltpu.VMEM((past_len // MOBA_BLOCK + 8, LANES), F32),
    ]
    if with_gate:
        scratch.append(pltpu.VMEM((past_len // MOBA_BLOCK, LANES), F32))
    return pl.pallas_call(
        kern,
        grid_spec=pltpu.PrefetchScalarGridSpec(
            num_scalar_prefetch=1,
            grid=(b * n_heads,),
            in_specs=[pl.BlockSpec(memory_space=pltpu.SMEM)] + [small(a) for a in small_inputs]
                     + [prompt, prompt, prompt, per_seq, per_seq, per_seq, hbm, hbm],
            out_specs=[prompt, per_seq],
            scratch_shapes=scratch,
        ),
        out_shape=[jax.ShapeDtypeStruct((b, t, width), BF16), jax.ShapeDtypeStruct((n_seq, dec, width), F32)],
        compiler_params=pltpu.CompilerParams(
            dimension_semantics=("arbitrary",), vmem_limit_bytes=VMEM_LIMIT),
        name=name,
    )(page_table, slopes, *small_inputs, q, k, v, qs, knew, vnew, pool_k, pool_v)


def _oproj_kernel(om_ref, od_ref, x_ref, w_ref, gpost_ref, gpre_ref, x1_ref, h2_ref):
    o = jnp.concatenate([om_ref[...].astype(BF16), od_ref[...].astype(BF16)], axis=1)
    x1 = x_ref[...] + _rms(_dot(o, w_ref[...]), gpost_ref[...])
    x1_ref[...] = x1
    h2_ref[...] = _rms(x1, gpre_ref[...]).astype(BF16)


def _oproj(om, od, x, w, gpost, gpre, tm):
    m, d = x.shape
    row = lambda width: pl.BlockSpec((tm, width), lambda i: (i, 0))
    const = lambda a: pl.BlockSpec(a.shape, lambda i: (0, 0))
    return pl.pallas_call(
        _oproj_kernel,
        grid=(m // tm,),
        in_specs=[row(om.shape[1]), row(od.shape[1]), row(d), const(w), const(gpost), const(gpre)],
        out_specs=[row(d), row(d)],
        out_shape=[jax.ShapeDtypeStruct((m, d), F32), jax.ShapeDtypeStruct((m, d), BF16)],
        compiler_params=pltpu.CompilerParams(
            dimension_semantics=("arbitrary",), vmem_limit_bytes=VMEM_LIMIT),
        name="oproj",
    )(om, od, x, w, gpost, gpre)


def _ffn_kernel(h_ref, x_ref, wg_ref, wu_ref, wd_ref, g_ref, o_ref, acc_ref):
    j = pl.program_id(1)

    @pl.when(j == 0)
    def _():
        acc_ref[...] = jnp.zeros(acc_ref.shape, F32)

    h = h_ref[...]
    gate = _dot(h, wg_ref[...])
    up = _dot(h, wu_ref[...])
    a = (gate * (1.0 / (1.0 + jnp.exp(-gate))) * up).astype(BF16)
    acc_ref[...] += _dot(a, wd_ref[...])

    @pl.when(j == pl.num_programs(1) - 1)
    def _():
        o_ref[...] = x_ref[...] + _rms(acc_ref[...], g_ref[...])


def _ffn(h, x, wg, wu, wd, g, tm, tf):
    m, d = x.shape
    f = wg.shape[1]
    return pl.pallas_call(
        _ffn_kernel,
        grid=(m // tm, f // tf),
        in_specs=[
            pl.BlockSpec((tm, d), lambda i, j: (i, 0)),
            pl.BlockSpec((tm, d), lambda i, j: (i, 0)),
            pl.BlockSpec((d, tf), lambda i, j: (0, j)),
            pl.BlockSpec((d, tf), lambda i, j: (0, j)),
            pl.BlockSpec((tf, d), lambda i, j: (j, 0)),
            pl.BlockSpec((1, d), lambda i, j: (0, 0)),
        ],
        out_specs=pl.BlockSpec((tm, d), lambda i, j: (i, 0)),
        out_shape=jax.ShapeDtypeStruct((m, d), F32),
        scratch_shapes=[pltpu.VMEM((tm, d), F32)],
        compiler_params=pltpu.CompilerParams(
            dimension_semantics=("arbitrary", "arbitrary"), vmem_limit_bytes=VMEM_LIMIT),
        name="ffn",
    )(h, x, wg, wu, wd, g)


def kernel(x_prompt, x_sample, cache_moba_k, cache_moba_v, cache_diff_k, cache_diff_v, page_table, g_pre_attn, w_in, lambda_q1, lambda_k1, lambda_q2, lambda_k2, g_subln, w_o, g_post_attn, g_pre_ffn, w_gate, w_up, w_down, g_post_ffn):
    batch, seq, d_model = x_prompt.shape
    dec_batch, dec_seq, _ = x_sample.shape
    depth, n_pool, page, h_moba, _ = cache_moba_k.shape
    h_diff = cache_diff_k.shape[3]
    n_heads = h_moba + h_diff
    wm, wd_ = h_moba * HEAD_DIM, h_diff * HEAD_DIM
    past_len = page_table.shape[1] * page
    assert seq % MOBA_BLOCK == 0 and past_len % MOBA_BLOCK == 0
    assert dec_seq <= MOBA_BLOCK and (page * PAGES_PER_STEP) % MOBA_BLOCK == 0
    assert wm == GROUP_WIDTH and wd_ == GROUP_WIDTH and h_moba * dec_seq <= LANES and 2 * h_diff * dec_seq <= LANES
    assert cache_diff_k.shape[4] == HEAD_DIM and cache_diff_v.shape[4] == HEAD_DIM

    slopes = _alibi_slopes(n_heads)
    s_moba, s_diff = slopes[0::2], slopes[1::2]
    pad_lanes = lambda v: jnp.pad(v, (0, LANES - v.shape[0])).reshape(1, LANES)
    lane_slope_m = pad_lanes(jnp.repeat(s_moba, dec_seq))
    lane_slope_d = pad_lanes(jnp.repeat(s_diff, 2 * dec_seq))
    pools = [c.reshape(depth, n_pool, -1, HEAD_DIM) for c in (cache_moba_k, cache_moba_v, cache_diff_k, cache_diff_v)]

    xp = x_prompt.reshape(batch * seq, d_model)
    xs = x_sample.reshape(dec_batch * dec_seq, d_model)
    tm_p, tm_s = 512, dec_batch * dec_seq
    kv_p = [[] for _ in range(4)]
    kv_s = [[] for _ in range(4)]
    for l in range(depth):
        lam_init = _lambda_init(l)
        row = lambda a: a[l].reshape(1, -1)
        lam_params = [row(lambda_q1), row(lambda_k1), row(lambda_q2), row(lambda_k2)]
        w_in_b, w_o_b = w_in[l].astype(BF16), w_o[l].astype(BF16)
        wg_b, wu_b, wd_b = w_gate[l].astype(BF16), w_up[l].astype(BF16), w_down[l].astype(BF16)

        proj_p = _proj(xp, row(g_pre_attn), w_in_b, 1024, 512)
        proj_s = _proj(xs, row(g_pre_attn), w_in_b, tm_s, GROUP_WIDTH)
        seq3 = lambda a: a.reshape(batch, seq, -1)
        dec3 = lambda a: a.reshape(dec_batch, dec_seq, -1)
        mq, mk, mv, dq, dk, dv = [seq3(a) for a in proj_p]
        mqs, mks, mvs, dqs, dks, dvs = [dec3(a) for a in proj_s]
        om_p, om_s = _attention(
            functools.partial(_moba_kernel, layer=l, n_heads=h_moba), "moba_attention", s_moba,
            [lane_slope_m], mq, mk, mv, mqs, mks, mvs, pools[0], pools[1], page_table, h_moba, True)
        od_p, od_s = _attention(
            functools.partial(_diff_kernel, layer=l, n_heads=h_diff, lam_init=lam_init), "diff_attention", s_diff,
            [lane_slope_d] + lam_params + [row(g_subln)], dq, dk, dv, dqs, dks, dvs, pools[2], pools[3],
            page_table, h_diff, False)

        x1, h2 = _oproj(om_p.reshape(batch * seq, wm), od_p.reshape(batch * seq, wd_), xp, w_o_b,
                        row(g_post_attn), row(g_pre_ffn), tm_p)
        xp = _ffn(h2, x1, wg_b, wu_b, wd_b, row(g_post_ffn), tm_p, 512)
        x1, h2 = _oproj(om_s.reshape(tm_s, wm), od_s.reshape(tm_s, wd_), xs, w_o_b,
                        row(g_post_attn), row(g_pre_ffn), tm_s)
        xs = _ffn(h2, x1, wg_b, wu_b, wd_b, row(g_post_ffn), tm_s, 512)
        for dst, a in zip(kv_p, (mk, mv, dk, dv)):
            dst.append(a.reshape(batch, seq, -1, HEAD_DIM))
        for dst, a in zip(kv_s, (mks, mvs, dks, dvs)):
            dst.append(a.reshape(dec_batch, dec_seq, -1, HEAD_DIM))

    return (xp.reshape(batch, seq, d_model), xs.reshape(dec_batch, dec_seq, d_model),
            *[jnp.stack(t) for t in kv_p], *[jnp.stack(t) for t in kv_s])
```

```python
import functools
import math

import jax
import jax.numpy as jnp
from jax import lax
from jax.experimental import pallas as pl
from jax.experimental.pallas import tpu as pltpu

F32 = jnp.float32
BF16 = jnp.bfloat16
NEG_INF = float("-inf")
LOG2_E = math.log2(math.e)

HEAD_DIM = 128
GROUP_WIDTH = 1024
LANES = 128
MXU_COLUMNS = 256
MOBA_BLOCK = 256
MOBA_TOPK = 3
MOBA_SCALE = HEAD_DIM ** -0.5
DIFF_SCALE = (HEAD_DIM // 2) ** -0.5
RMS_EPS = 1e-6
PAGES_PER_STEP = 8
RING_SLOTS = 4
VMEM_LIMIT = 56 * 1024 * 1024


def _alibi_slopes(n_heads):
    return 2.0 ** (-8.0 * jnp.arange(1, n_heads + 1, dtype=F32) / n_heads)


def _lambda_init(layer):
    return 0.8 - 0.6 * math.exp(-0.3 * layer)


def _rms(x, g):
    return (x * lax.rsqrt(jnp.mean(x * x, axis=-1, keepdims=True) + RMS_EPS)) * g


def _dot(a, b):
    return jnp.dot(a, b, preferred_element_type=F32)


def _dot_row_slabs(a, b, slabs=2):
    if b.shape[1] > MXU_COLUMNS or slabs == 1:
        return _dot(a, b)
    rows = a.shape[0] // slabs
    return jnp.concatenate([_dot(a[i * rows:(i + 1) * rows, :], b) for i in range(slabs)], axis=0)


def _pad_rows(a, rows):
    return jnp.concatenate([a, jnp.zeros((rows - a.shape[0], a.shape[1]), a.dtype)], axis=0)


def _proj_kernel(x_ref, g_ref, w_ref, *refs):
    outs, h_ref = refs[:-1], refs[-1]
    j = pl.program_id(1)

    @pl.when(j == 0)
    def _():
        h_ref[...] = _rms(x_ref[...], g_ref[...]).astype(BF16)

    per_out = GROUP_WIDTH // outs[0].shape[1]
    for k, o_ref in enumerate(outs):
        @pl.when((j >= k * per_out) & (j < (k + 1) * per_out))
        def _(o_ref=o_ref):
            o_ref[...] = _dot(h_ref[...], w_ref[...])


def _proj(x, g, w, tm, tn):
    m, d = x.shape
    n_out = w.shape[1] // GROUP_WIDTH
    per_out = GROUP_WIDTH // tn
    last_i = m // tm - 1

    def out_map(k):
        def index(i, j):
            done = j >= (k + 1) * per_out
            move = done & (i < last_i)
            col = jnp.where(done, jnp.where(move, 0, per_out - 1), jnp.clip(j - k * per_out, 0, per_out - 1))
            return jnp.where(move, i + 1, i), col
        return index

    return pl.pallas_call(
        _proj_kernel,
        grid=(m // tm, n_out * per_out),
        in_specs=[
            pl.BlockSpec((tm, d), lambda i, j: (jnp.minimum(i + jnp.minimum(j, 1), last_i), 0)),
            pl.BlockSpec((1, d), lambda i, j: (0, 0)),
            pl.BlockSpec((d, tn), lambda i, j: (0, j)),
        ],
        out_specs=[pl.BlockSpec((tm, tn), out_map(k)) for k in range(n_out)],
        out_shape=[jax.ShapeDtypeStruct((m, GROUP_WIDTH), F32) for _ in range(n_out)],
        scratch_shapes=[pltpu.VMEM((tm, d), BF16)],
        compiler_params=pltpu.CompilerParams(
            dimension_semantics=("arbitrary", "arbitrary"), vmem_limit_bytes=VMEM_LIMIT),
        name="proj",
    )(x, g, w)


def _causal_bias(maps):
    t = lax.broadcasted_iota(jnp.int32, (MOBA_BLOCK, MOBA_BLOCK), 1)
    j = lax.broadcasted_iota(jnp.int32, (MOBA_BLOCK, MOBA_BLOCK), 0)
    c = jnp.where(j <= t, 0.0, NEG_INF)
    return c if maps == 1 else jnp.concatenate([c] * maps, axis=1)


def _bf16_part(w):
    bits = lax.bitcast_convert_type(w, jnp.uint32) & jnp.uint32(0xFFFF0000)
    return lax.bitcast_convert_type(bits, F32)


def _split3(w):
    hi = _bf16_part(w)
    mid = _bf16_part(w - hi)
    return hi, mid, w - hi - mid


def _key_bias_columns(key_weights, seq):
    w = key_weights[:, None] * jnp.arange(seq, dtype=F32)[None, :]
    cols = jnp.stack(_split3(w), axis=-1)
    return jnp.pad(cols, ((0, 0), (0, 0), (0, HEAD_DIM - cols.shape[-1]))).astype(BF16)


def _stage_kv(k_ref, v_ref, kb_ref, vt_ref):
    blocks = []
    for n in range(k_ref.shape[0] // MOBA_BLOCK):
        rows = slice(n * MOBA_BLOCK, (n + 1) * MOBA_BLOCK)
        kblk = k_ref[rows, :]
        kb_ref[rows, :] = kblk.astype(BF16)
        vt_ref[:, rows] = jnp.transpose(v_ref[rows, :]).astype(BF16)
        blocks.append(kblk)
    return blocks


def _with_bias_rows(qt):
    row = lax.broadcasted_iota(jnp.int32, qt.shape, 0)
    return jnp.concatenate([qt, jnp.where(row < 3, 1.0, 0.0).astype(BF16)], axis=0)


def _attend_block(i, qt, kb_ref, kbias_ref, vt_ref, scale2, causal, offsets, emit):
    nk = (i + 1) * MOBA_BLOCK
    raw = _dot_row_slabs(jnp.concatenate([kb_ref[0:nk, :], kbias_ref[0:nk, :]], axis=1), _with_bias_rows(qt),
                         2 if i else 1)
    yield
    ts, m = [], None
    for n in range(i + 1):
        t = raw[n * MOBA_BLOCK:(n + 1) * MOBA_BLOCK, :]
        if scale2 != 1.0:
            t = t * scale2
        if n == i:
            t = t + causal
        mb = jnp.max(t, axis=0, keepdims=True)
        if n < i and offsets:
            mb = mb + offsets[n]
        m = mb if m is None else jnp.maximum(m, mb)
        ts.append(t)
    yield
    ps, l = [], None
    for n in range(i + 1):
        p = jnp.exp2(ts[n] - (m - offsets[n] if n < i and offsets else m))
        ps.append(p.astype(BF16))
        pl_sum = jnp.sum(p, axis=0, keepdims=True)
        l = pl_sum if l is None else l + pl_sum
    emit(_dot(vt_ref[:, 0:nk], jnp.concatenate(ps, axis=0)) / l)


def _first_stage(stream):
    next(stream)
    return
    yield


def _interleave(*streams):
    streams = [s for s in streams if s is not None]
    while streams:
        for s in list(streams):
            try:
                next(s)
            except StopIteration:
                streams.remove(s)


def _moba_prompt_blocks(q_ref, k_ref, v_ref, kbias_ref, o_ref, kb_ref, vt_ref):
    nb = q_ref.shape[0] // MOBA_BLOCK
    kblocks = _stage_kv(k_ref, v_ref, kb_ref, vt_ref)
    km = jnp.concatenate([jnp.mean(kb, axis=0, keepdims=True) for kb in kblocks], axis=0)
    km_hi = km.astype(BF16)
    km_lo = (km - km_hi.astype(F32)).astype(BF16)

    def block(i):
        blk = lax.broadcasted_iota(jnp.int32, (nb, MOBA_BLOCK), 0)
        rows = slice(i * MOBA_BLOCK, (i + 1) * MOBA_BLOCK)
        qt = jnp.transpose(q_ref[rows, :]).astype(BF16)
        offsets = []
        if i > 0:
            gate = _dot(km_hi, qt) + _dot(km_lo, qt)
            cnt = jnp.zeros(gate.shape, F32)
            for m in range(i):
                gm = gate[m:m + 1, :]
                beats = (gm > gate) | ((gm == gate) & (m < blk))
                cnt = cnt + jnp.where(beats, 1.0, 0.0)
            selb = jnp.where((blk < i) & (cnt < MOBA_TOPK), 0.0, NEG_INF)
            offsets = [selb[n:n + 1, :] for n in range(i)]

        def emit(o):
            o_ref[rows, :] = jnp.transpose(o).astype(o_ref.dtype)

        yield from _attend_block(i, qt, kb_ref, kbias_ref, vt_ref, MOBA_SCALE * LOG2_E, _causal_bias(1),
                                 offsets, emit)

    return block


def _lambda(lam_refs, lam_init):
    lq1_ref, lk1_ref, lq2_ref, lk2_ref = lam_refs
    a = jnp.sum(lq1_ref[...] * lk1_ref[...], axis=-1, keepdims=True)
    b = jnp.sum(lq2_ref[...] * lk2_ref[...], axis=-1, keepdims=True)
    return jnp.exp(a) - jnp.exp(b) + lam_init


def _diff_prompt_blocks(lam, lam_init, gsub_ref, q_ref, k_ref, v_ref, kbias_ref, o_ref, kb_ref, vt_ref):
    qk_dim = HEAD_DIM // 2
    _stage_kv(k_ref, v_ref, kb_ref, vt_ref)

    def block(i):
        row = lax.broadcasted_iota(jnp.int32, (HEAD_DIM, MOBA_BLOCK), 0)
        rows = slice(i * MOBA_BLOCK, (i + 1) * MOBA_BLOCK)
        qt = jnp.transpose(q_ref[rows, :]) * (DIFF_SCALE * LOG2_E)
        qt2 = jnp.concatenate([jnp.where(row < qk_dim, qt, 0.0), jnp.where(row >= qk_dim, qt, 0.0)],
                              axis=1).astype(BF16)

        def emit(on):
            o = jnp.transpose(on[:, :MOBA_BLOCK] - lam * on[:, MOBA_BLOCK:])
            o_ref[rows, :] = (_rms(o, gsub_ref[...]) * (1.0 - lam_init)).astype(o_ref.dtype)

        yield from _attend_block(i, qt2, kb_ref, kbias_ref, vt_ref, 1.0, _causal_bias(2), [], emit)

    return block


class _PageRing:
    def __init__(self, pt_ref, kpool, vpool, ring, sem, layer, n_heads):
        self.pt_ref, self.kpool, self.vpool, self.ring, self.sem = pt_ref, kpool, vpool, ring, sem
        self.layer, self.n_heads = layer, n_heads
        self.page_rows = kpool.shape[2]
        self.keys = self.page_rows // n_heads
        self.steps = pt_ref.shape[1] // PAGES_PER_STEP
        self.ahead = RING_SLOTS - 1
        assert self.steps % RING_SLOTS == 0 and self.ahead < self.steps
        self.seq = pl.program_id(0) // 2
        self.n_seq = pl.num_programs(0) // 2

    def _copy(self, pool, page_idx, slot, gi):
        return pltpu.make_async_copy(pool.at[self.layer, page_idx],
                                     self.ring.at[slot, pl.ds(gi * self.page_rows, self.page_rows)],
                                     self.sem.at[slot])

    def _start(self, seq, sweep, step):
        pool = self.vpool if sweep else self.kpool
        for gi in range(PAGES_PER_STEP):
            self._copy(pool, self.pt_ref[seq, step * PAGES_PER_STEP + gi], step % RING_SLOTS, gi).start()

    def prologue(self):
        for step in range(self.ahead):
            self._start(0, 0, step)

    def advance(self, sweep, step):
        later = sweep * self.steps + step + self.ahead
        if later < 2 * self.steps:
            self._start(self.seq, later // self.steps, later % self.steps)
        else:
            @pl.when(self.seq + 1 < self.n_seq)
            def _():
                self._start(self.seq + 1, 0, later - 2 * self.steps)
        for gi in range(PAGES_PER_STEP):
            self._copy(self.kpool, 0, step % RING_SLOTS, gi).wait()

    def pages(self, step):
        slot = step % RING_SLOTS
        return jnp.concatenate([
            jnp.concatenate(
                [self.ring[slot, pl.ds(gi * self.page_rows + h, self.keys, stride=self.n_heads), :].astype(BF16)
                 for h in range(self.n_heads)], axis=1)
            for gi in range(PAGES_PER_STEP)], axis=0)


def _block_diag_q(q, groups, qbd_ref):
    dec, width = q.shape
    gd = width // groups
    qt = jnp.transpose(_pad_rows(q, LANES))
    for gi in range(groups):
        rows = slice(gi * gd, (gi + 1) * gd)
        blk = qt[rows, :]
        if gi:
            blk = pltpu.roll(blk, gi * dec, axis=1)
        qbd_ref[rows, :] = blk.astype(BF16)


class _SampleScores:
    def __init__(self, s_ref, stat_ref, l_ref, acc_ref, qbd_ref, lane_slope_ref, scale, past_len, dec):
        self.s_ref, self.stat_ref, self.l_ref, self.acc_ref, self.qbd_ref = s_ref, stat_ref, l_ref, acc_ref, qbd_ref
        self.scale, self.past_len = scale, past_len
        self.n_past_blk = past_len // MOBA_BLOCK
        self.lane_slope = lane_slope_ref[...]
        self.q_pos = past_len + lax.broadcasted_iota(jnp.int32, (1, LANES), 1) % dec

    def _rows(self, blk):
        return slice(blk * MOBA_BLOCK, (blk + 1) * MOBA_BLOCK)

    def _store(self, blk, logits):
        self.stat_ref[blk:blk + 1, :] = jnp.max(logits, axis=0, keepdims=True)
        self.s_ref[self._rows(blk), :] = logits

    def store_cached(self, blk, raw):
        in_block = lax.broadcasted_iota(jnp.int32, (MOBA_BLOCK, LANES), 0).astype(F32)
        first = -self.lane_slope * (self.q_pos - blk * MOBA_BLOCK).astype(F32)
        scaled = raw if self.scale == 1.0 else raw * self.scale
        self._store(blk, scaled + self.lane_slope * in_block + first)

    def store_new(self, knew):
        raw = _dot_row_slabs(_pad_rows(knew, MOBA_BLOCK).astype(BF16), self.qbd_ref[...])
        k_pos = self.past_len + lax.broadcasted_iota(jnp.int32, (MOBA_BLOCK, 1), 0)
        dist = self.q_pos - k_pos
        scaled = raw if self.scale == 1.0 else raw * self.scale
        self._store(self.n_past_blk, jnp.where(dist >= 0, scaled - self.lane_slope * dist.astype(F32), NEG_INF))

    def finish(self, selb=None):
        n = self.n_past_blk
        past = self.stat_ref[0:n, :] if selb is None else self.stat_ref[0:n, :] + selb
        m = jnp.maximum(jnp.max(past, axis=0, keepdims=True), self.stat_ref[n:n + 1, :])
        self.stat_ref[0:n, :] = jnp.broadcast_to(-m, past.shape) if selb is None else selb - m
        self.stat_ref[n:n + 1, :] = -m

    def pv(self, blocks, values, first):
        vb = values()
        yield
        l = jnp.zeros((1, LANES), F32) if first else self.l_ref[0:1, :]
        ps = []
        for blk in blocks:
            p = jnp.exp(self.s_ref[self._rows(blk), :] + self.stat_ref[blk:blk + 1, :])
            l = l + jnp.sum(p, axis=0, keepdims=True)
            ps.append(p)
        self.l_ref[0:1, :] = l
        pt = jnp.transpose(jnp.concatenate(ps, axis=0)).astype(BF16)
        yield
        part = _dot(pt, vb)
        self.acc_ref[...] = part if first else self.acc_ref[...] + part

    def row_scale(self):
        return jnp.transpose(jnp.broadcast_to(1.0 / self.l_ref[0:1, :], (LANES, LANES)))


def _sweeps(ring, prompt_block, n_prompt_blocks, k_step, k_done, v_step, v_done):
    g = pl.program_id(0)

    @pl.when(g == 0)
    def _():
        ring.prologue()

    for sweep, step_fn, done_fn in ((0, k_step, k_done), (1, v_step, v_done)):
        @pl.when(g % 2 == sweep)
        def _(sweep=sweep, step_fn=step_fn, done_fn=done_fn):
            blocks = [prompt_block(i) for i in range(n_prompt_blocks)]
            next(blocks[0])
            for r in range(max(ring.steps, n_prompt_blocks)):
                if r < ring.steps:
                    ring.advance(sweep, r)
                _interleave(step_fn(r) if r < ring.steps else None,
                            blocks[r] if r < n_prompt_blocks else None,
                            _first_stage(blocks[r + 1]) if r + 1 < n_prompt_blocks else None)
            done_fn()


def _moba_kernel(pt_ref, lane_slope_ref, q_ref, k_ref, v_ref, kbias_ref, qs_ref, knew_ref, vnew_ref,
                 kpool, vpool, o_ref, os_ref, kb_ref, vt_ref, s_ref, acc_ref, l_ref, qbd_ref, ring_ref, sem,
                 stat_ref, gate_ref, *, layer, n_heads):
    ring = _PageRing(pt_ref, kpool, vpool, ring_ref, sem, layer, n_heads)
    past_len = ring.steps * PAGES_PER_STEP * ring.keys
    blocks_per_step = PAGES_PER_STEP * ring.keys // MOBA_BLOCK
    n_past_blk = past_len // MOBA_BLOCK
    dec = knew_ref.shape[0]
    prompt_block = _moba_prompt_blocks(q_ref, k_ref, v_ref, kbias_ref, o_ref, kb_ref, vt_ref)
    scores = _SampleScores(s_ref, stat_ref, l_ref, acc_ref, qbd_ref, lane_slope_ref, MOBA_SCALE, past_len, dec)

    def k_step(r):
        if r == 0:
            _block_diag_q(qs_ref[...], n_heads, qbd_ref)
        pages = ring.pages(r)
        yield
        sc = _dot_row_slabs(pages, qbd_ref[...])
        yield
        for b in range(blocks_per_step):
            blk = r * blocks_per_step + b
            raw = sc[b * MOBA_BLOCK:(b + 1) * MOBA_BLOCK, :]
            gate_ref[blk:blk + 1, :] = jnp.sum(raw, axis=0, keepdims=True) * (1.0 / MOBA_BLOCK)
            scores.store_cached(blk, raw)

    def k_done():
        scores.store_new(knew_ref[...])
        gate = gate_ref[0:n_past_blk, :]
        blk = lax.broadcasted_iota(jnp.int32, gate.shape, 0)
        cnt = jnp.zeros(gate.shape, F32)
        for m in range(n_past_blk):
            gm = gate[m:m + 1, :]
            beats = (gm > gate) | ((gm == gate) & (m < blk))
            cnt = cnt + jnp.where(beats, 1.0, 0.0)
        scores.finish(jnp.where(cnt < MOBA_TOPK, 0.0, NEG_INF))

    def v_step(r):
        return scores.pv(range(r * blocks_per_step, (r + 1) * blocks_per_step), lambda: ring.pages(r), r == 0)

    def v_done():
        _interleave(scores.pv([n_past_blk], lambda: _pad_rows(vnew_ref[...], MOBA_BLOCK).astype(BF16), False))
        acc, inv_l = acc_ref[...], scores.row_scale()
        for h in range(n_heads):
            cols, lanes = slice(h * HEAD_DIM, (h + 1) * HEAD_DIM), slice(h * dec, (h + 1) * dec)
            os_ref[:, cols] = acc[lanes, cols] * inv_l[lanes, :]

    _sweeps(ring, prompt_block, q_ref.shape[0] // MOBA_BLOCK, k_step, k_done, v_step, v_done)


def _diff_kernel(pt_ref, lane_slope_ref, lq1_ref, lk1_ref, lq2_ref, lk2_ref, gsub_ref,
                 q_ref, k_ref, v_ref, kbias_ref, qs_ref, knew_ref, vnew_ref, kpool, vpool, o_ref, os_ref,
                 kb_ref, vt_ref, s_ref, acc_ref, l_ref, qbd_ref, ring_ref, sem, stat_ref,
                 *, layer, n_heads, lam_init):
    ring = _PageRing(pt_ref, kpool, vpool, ring_ref, sem, layer, n_heads)
    past_len = ring.steps * PAGES_PER_STEP * ring.keys
    blocks_per_step = PAGES_PER_STEP * ring.keys // MOBA_BLOCK
    dec = knew_ref.shape[0]
    lam = _lambda((lq1_ref, lk1_ref, lq2_ref, lk2_ref), lam_init)
    prompt_block = _diff_prompt_blocks(lam, lam_init, gsub_ref, q_ref, k_ref, v_ref, kbias_ref, o_ref, kb_ref,
                                       vt_ref)
    scores = _SampleScores(s_ref, stat_ref, l_ref, acc_ref, qbd_ref, lane_slope_ref, 1.0, past_len, dec)

    def k_step(r):
        if r == 0:
            _block_diag_q(qs_ref[...] * DIFF_SCALE, 2 * n_heads, qbd_ref)
        pages = ring.pages(r)
        yield
        sc = _dot_row_slabs(pages, qbd_ref[...])
        yield
        for b in range(blocks_per_step):
            scores.store_cached(r * blocks_per_step + b, sc[b * MOBA_BLOCK:(b + 1) * MOBA_BLOCK, :])

    def k_done():
        scores.store_new(knew_ref[...])
        scores.finish()

    def v_step(r):
        return scores.pv(range(r * blocks_per_step, (r + 1) * blocks_per_step), lambda: ring.pages(r), r == 0)

    def v_done():
        _interleave(scores.pv([scores.n_past_blk], lambda: _pad_rows(vnew_ref[...], MOBA_BLOCK).astype(BF16),
                              False))
        acc, inv_l = acc_ref[...], scores.row_scale()
        for h in range(n_heads):
            cols = slice(h * HEAD_DIM, (h + 1) * HEAD_DIM)
            map1, map2 = slice(2 * h * dec, (2 * h + 1) * dec), slice((2 * h + 1) * dec, (2 * h + 2) * dec)
            o = acc[map1, cols] * inv_l[map1, :] - lam * (acc[map2, cols] * inv_l[map2, :])
            os_ref[:, cols] = _rms(o, gsub_ref[...]) * (1.0 - lam_init)

    _sweeps(ring, prompt_block, q_ref.shape[0] // MOBA_BLOCK, k_step, k_done, v_step, v_done)


def _attention(kern, name, key_weights, small_inputs, q, k, v, qs, knew, vnew, pool_k, pool_v, page_table,
               n_heads, with_gate):
    b, t, width = q.shape
    n_seq, n_pages = page_table.shape
    page_rows, hd = pool_k.shape[2], pool_k.shape[3]
    dec = qs.shape[1]
    past_len = n_pages * (page_rows // n_heads)
    assert b * n_heads == 2 * n_seq, "one prompt (batch, head) and one sample sweep per grid step"
    prompt = pl.BlockSpec((None, t, HEAD_DIM), lambda g, pt: (g // n_heads, 0, g % n_heads))
    per_seq = pl.BlockSpec((None, dec, width), lambda g, pt: (g // 2, 0, 0))
    small = lambda a: pl.BlockSpec(a.shape, lambda g, pt: (0,) * a.ndim)
    per_head = pl.BlockSpec((None, t, HEAD_DIM), lambda g, pt: (g % n_heads, 0, 0))
    hbm = pl.BlockSpec(memory_space=pl.ANY)
    scratch = [
        pltpu.VMEM((t, HEAD_DIM), BF16),
        pltpu.VMEM((HEAD_DIM, t), BF16),
        pltpu.VMEM((past_len + MOBA_BLOCK, LANES), F32),
        pltpu.VMEM((LANES, width), F32),
        pltpu.VMEM((8, LANES), F32),
        pltpu.VMEM((width, LANES), BF16),
        pltpu.VMEM((RING_SLOTS, PAGES_PER_STEP * page_rows, hd), pool_k.dtype),
        pltpu.SemaphoreType.DMA((RING_SLOTS,)),
        pltpu.VMEM((past_len // MOBA_BLOCK + 8, LANES), F32),
    ]
    if with_gate:
        scratch.append(pltpu.VMEM((past_len // MOBA_BLOCK, LANES), F32))
    return pl.pallas_call(
        kern,
        grid_spec=pltpu.PrefetchScalarGridSpec(
            num_scalar_prefetch=1,
            grid=(b * n_heads,),
            in_specs=[small(a) for a in small_inputs]
                     + [prompt, prompt, prompt, per_head, per_seq, per_seq, per_seq, hbm, hbm],
            out_specs=[prompt, per_seq],
            scratch_shapes=scratch,
        ),
        out_shape=[jax.ShapeDtypeStruct((b, t, width), BF16), jax.ShapeDtypeStruct((n_seq, dec, width), F32)],
        compiler_params=pltpu.CompilerParams(
            dimension_semantics=("arbitrary",), vmem_limit_bytes=VMEM_LIMIT),
        name=name,
    )(page_table, *small_inputs, q, k, v, _key_bias_columns(key_weights, t), qs, knew, vnew, pool_k, pool_v)


def _oproj_kernel(om_ref, od_ref, x_ref, w_ref, gpost_ref, gpre_ref, x1_ref, h2_ref):
    o = jnp.concatenate([om_ref[...].astype(BF16), od_ref[...].astype(BF16)], axis=1)
    x1 = x_ref[...] + _rms(_dot(o, w_ref[...]), gpost_ref[...])
    x1_ref[...] = x1
    h2_ref[...] = _rms(x1, gpre_ref[...]).astype(BF16)


def _oproj(om, od, x, w, gpost, gpre, tm):
    m, d = x.shape
    row = lambda width: pl.BlockSpec((tm, width), lambda i: (i, 0))
    const = lambda a: pl.BlockSpec(a.shape, lambda i: (0, 0))
    return pl.pallas_call(
        _oproj_kernel,
        grid=(m // tm,),
        in_specs=[row(om.shape[1]), row(od.shape[1]), row(d), const(w), const(gpost), const(gpre)],
        out_specs=[row(d), row(d)],
        out_shape=[jax.ShapeDtypeStruct((m, d), F32), jax.ShapeDtypeStruct((m, d), BF16)],
        compiler_params=pltpu.CompilerParams(
            dimension_semantics=("arbitrary",), vmem_limit_bytes=VMEM_LIMIT),
        name="oproj",
    )(om, od, x, w, gpost, gpre)


def _ffn_kernel(h_ref, x_ref, wg_ref, wu_ref, wd_ref, g_ref, o_ref, acc_ref):
    j = pl.program_id(1)

    @pl.when(j == 0)
    def _():
        acc_ref[...] = jnp.zeros(acc_ref.shape, F32)

    h = h_ref[...]
    gate = _dot(h, wg_ref[...])
    up = _dot(h, wu_ref[...])
    a = (gate * (1.0 / (1.0 + jnp.exp(-gate))) * up).astype(BF16)
    acc_ref[...] += _dot(a, wd_ref[...])

    @pl.when(j == pl.num_programs(1) - 1)
    def _():
        o_ref[...] = x_ref[...] + _rms(acc_ref[...], g_ref[...])


def _ffn(h, x, wg, wu, wd, g, tm, tf):
    m, d = x.shape
    f = wg.shape[1]
    return pl.pallas_call(
        _ffn_kernel,
        grid=(m // tm, f // tf),
        in_specs=[
            pl.BlockSpec((tm, d), lambda i, j: (i, 0)),
            pl.BlockSpec((tm, d), lambda i, j: (i, 0)),
            pl.BlockSpec((d, tf), lambda i, j: (0, j)),
            pl.BlockSpec((d, tf), lambda i, j: (0, j)),
            pl.BlockSpec((tf, d), lambda i, j: (j, 0)),
            pl.BlockSpec((1, d), lambda i, j: (0, 0)),
        ],
        out_specs=pl.BlockSpec((tm, d), lambda i, j: (i, 0)),
        out_shape=jax.ShapeDtypeStruct((m, d), F32),
        scratch_shapes=[pltpu.VMEM((tm, d), F32)],
        compiler_params=pltpu.CompilerParams(
            dimension_semantics=("arbitrary", "arbitrary"), vmem_limit_bytes=VMEM_LIMIT),
        name="ffn",
    )(h, x, wg, wu, wd, g)


def kernel(x_prompt, x_sample, cache_moba_k, cache_moba_v, cache_diff_k, cache_diff_v, page_table, g_pre_attn, w_in, lambda_q1, lambda_k1, lambda_q2, lambda_k2, g_subln, w_o, g_post_attn, g_pre_ffn, w_gate, w_up, w_down, g_post_ffn):
    batch, seq, d_model = x_prompt.shape
    dec_batch, dec_seq, _ = x_sample.shape
    depth, n_pool, page, h_moba, _ = cache_moba_k.shape
    h_diff = cache_diff_k.shape[3]
    n_heads = h_moba + h_diff
    wm, wd_ = h_moba * HEAD_DIM, h_diff * HEAD_DIM
    past_len = page_table.shape[1] * page
    assert seq % MOBA_BLOCK == 0 and past_len % MOBA_BLOCK == 0
    assert dec_seq <= MOBA_BLOCK and (page * PAGES_PER_STEP) % MOBA_BLOCK == 0
    assert wm == GROUP_WIDTH and wd_ == GROUP_WIDTH and h_moba * dec_seq <= LANES and 2 * h_diff * dec_seq <= LANES
    assert cache_diff_k.shape[4] == HEAD_DIM and cache_diff_v.shape[4] == HEAD_DIM

    slopes = _alibi_slopes(n_heads)
    s_moba, s_diff = slopes[0::2], slopes[1::2]
    pad_lanes = lambda v: jnp.pad(v, (0, LANES - v.shape[0])).reshape(1, LANES)
    lane_slope_m = pad_lanes(jnp.repeat(s_moba, dec_seq))
    lane_slope_d = pad_lanes(jnp.repeat(s_diff, 2 * dec_seq))
    pools = [c.reshape(depth, n_pool, -1, HEAD_DIM) for c in (cache_moba_k, cache_moba_v, cache_diff_k, cache_diff_v)]

    xp = x_prompt.reshape(batch * seq, d_model)
    xs = x_sample.reshape(dec_batch * dec_seq, d_model)
    tm_p, tm_s = 512, dec_batch * dec_seq
    kv_p = [[] for _ in range(4)]
    kv_s = [[] for _ in range(4)]
    for l in range(depth):
        lam_init = _lambda_init(l)
        row = lambda a: a[l].reshape(1, -1)
        lam_params = [row(lambda_q1), row(lambda_k1), row(lambda_q2), row(lambda_k2)]
        w_in_b, w_o_b = w_in[l].astype(BF16), w_o[l].astype(BF16)
        wg_b, wu_b, wd_b = w_gate[l].astype(BF16), w_up[l].astype(BF16), w_down[l].astype(BF16)

        proj_p = _proj(xp, row(g_pre_attn), w_in_b, 1024, 512)
        proj_s = _proj(xs, row(g_pre_attn), w_in_b, tm_s, GROUP_WIDTH)
        seq3 = lambda a: a.reshape(batch, seq, -1)
        dec3 = lambda a: a.reshape(dec_batch, dec_seq, -1)
        mq, mk, mv, dq, dk, dv = [seq3(a) for a in proj_p]
        mqs, mks, mvs, dqs, dks, dvs = [dec3(a) for a in proj_s]
        om_p, om_s = _attention(
            functools.partial(_moba_kernel, layer=l, n_heads=h_moba), "moba_attention", s_moba * (1.0 / MOBA_SCALE),
            [lane_slope_m], mq, mk, mv, mqs, mks, mvs, pools[0], pools[1], page_table, h_moba, True)
        od_p, od_s = _attention(
            functools.partial(_diff_kernel, layer=l, n_heads=h_diff, lam_init=lam_init), "diff_attention",
            s_diff * LOG2_E,
            [lane_slope_d] + lam_params + [row(g_subln)], dq, dk, dv, dqs, dks, dvs, pools[2], pools[3],
            page_table, h_diff, False)

        x1, h2 = _oproj(om_p.reshape(batch * seq, wm), od_p.reshape(batch * seq, wd_), xp, w_o_b,
                        row(g_post_attn), row(g_pre_ffn), tm_p)
        xp = _ffn(h2, x1, wg_b, wu_b, wd_b, row(g_post_ffn), tm_p, 512)
        x1, h2 = _oproj(om_s.reshape(tm_s, wm), od_s.reshape(tm_s, wd_), xs, w_o_b,
                        row(g_post_attn), row(g_pre_ffn), tm_s)
        xs = _ffn(h2, x1, wg_b, wu_b, wd_b, row(g_post_ffn), tm_s, 512)
        for dst, a in zip(kv_p, (mk, mv, dk, dv)):
            dst.append(a.reshape(batch, seq, -1, HEAD_DIM))
        for dst, a in zip(kv_s, (mks, mvs, dks, dvs)):
            dst.append(a.reshape(dec_batch, dec_seq, -1, HEAD_DIM))

    return (xp.reshape(batch, seq, d_model), xs.reshape(dec_batch, dec_seq, d_model),
            *[jnp.stack(t) for t in kv_p], *[jnp.stack(t) for t in kv_s])
```

```python
import functools
import math

import jax
import jax.numpy as jnp
from jax import lax
from jax.experimental import pallas as pl
from jax.experimental.pallas import tpu as pltpu

F32 = jnp.float32
BF16 = jnp.bfloat16
NEG_INF = float("-inf")
LOG2_E = math.log2(math.e)

HEAD_DIM = 128
GROUP_WIDTH = 1024
LANES = 128
MXU_COLUMNS = 256
MOBA_BLOCK = 256
MOBA_TOPK = 3
MOBA_SCALE = HEAD_DIM ** -0.5
DIFF_SCALE = (HEAD_DIM // 2) ** -0.5
RMS_EPS = 1e-6
PAGES_PER_STEP = 8
RING_SLOTS = 4
OPROJ_SLABS = 2
PROJ_SLABS = 4
VMEM_LIMIT = 56 * 1024 * 1024


def _alibi_slopes(n_heads):
    return 2.0 ** (-8.0 * jnp.arange(1, n_heads + 1, dtype=F32) / n_heads)


def _lambda_init(layer):
    return 0.8 - 0.6 * math.exp(-0.3 * layer)


def _rms(x, g):
    return (x * lax.rsqrt(jnp.mean(x * x, axis=-1, keepdims=True) + RMS_EPS)) * g


def _dot(a, b):
    return jnp.dot(a, b, preferred_element_type=F32)


def _dot_row_slabs(a, b, slabs=2):
    if b.shape[1] > MXU_COLUMNS or slabs == 1:
        return _dot(a, b)
    rows = a.shape[0] // slabs
    return jnp.concatenate([_dot(a[i * rows:(i + 1) * rows, :], b) for i in range(slabs)], axis=0)


def _pad_rows(a, rows):
    return jnp.concatenate([a, jnp.zeros((rows - a.shape[0], a.shape[1]), a.dtype)], axis=0)


def _proj_kernel(x_ref, g_ref, w_ref, *refs):
    outs, h_ref = refs[:-1], refs[-1]
    j = pl.program_id(1)

    @pl.when(j == 0)
    def _():
        slab = x_ref.shape[0] // PROJ_SLABS
        for s in range(PROJ_SLABS):
            rows = slice(s * slab, (s + 1) * slab)
            h = _rms(x_ref[rows, :], g_ref[...]).astype(BF16)
            h_ref[rows, :] = h
            outs[0][rows, :] = _dot(h, w_ref[...])

    per_out = GROUP_WIDTH // outs[0].shape[1]
    for k, o_ref in enumerate(outs):
        @pl.when((j >= max(k * per_out, 1)) & (j < (k + 1) * per_out))
        def _(o_ref=o_ref):
            o_ref[...] = _dot(h_ref[...], w_ref[...])


def _proj(x, g, w, tm, tn):
    m, d = x.shape
    n_out = w.shape[1] // GROUP_WIDTH
    per_out = GROUP_WIDTH // tn
    last_i = m // tm - 1

    def out_map(k):
        def index(i, j):
            done = j >= (k + 1) * per_out
            move = done & (i < last_i)
            col = jnp.where(done, jnp.where(move, 0, per_out - 1), jnp.clip(j - k * per_out, 0, per_out - 1))
            return jnp.where(move, i + 1, i), col
        return index

    return pl.pallas_call(
        _proj_kernel,
        grid=(m // tm, n_out * per_out),
        in_specs=[
            pl.BlockSpec((tm, d), lambda i, j: (jnp.minimum(i + jnp.minimum(j, 1), last_i), 0)),
            pl.BlockSpec((1, d), lambda i, j: (0, 0)),
            pl.BlockSpec((d, tn), lambda i, j: (0, j)),
        ],
        out_specs=[pl.BlockSpec((tm, tn), out_map(k)) for k in range(n_out)],
        out_shape=[jax.ShapeDtypeStruct((m, GROUP_WIDTH), F32) for _ in range(n_out)],
        scratch_shapes=[pltpu.VMEM((tm, d), BF16)],
        compiler_params=pltpu.CompilerParams(
            dimension_semantics=("arbitrary", "arbitrary"), vmem_limit_bytes=VMEM_LIMIT),
        name="proj",
    )(x, g, w)


def _causal_bias(maps):
    t = lax.broadcasted_iota(jnp.int32, (MOBA_BLOCK, MOBA_BLOCK), 1)
    j = lax.broadcasted_iota(jnp.int32, (MOBA_BLOCK, MOBA_BLOCK), 0)
    c = jnp.where(j <= t, 0.0, NEG_INF)
    return c if maps == 1 else jnp.concatenate([c] * maps, axis=1)


def _bf16_part(w):
    bits = lax.bitcast_convert_type(w, jnp.uint32) & jnp.uint32(0xFFFF0000)
    return lax.bitcast_convert_type(bits, F32)


def _split3(w):
    hi = _bf16_part(w)
    mid = _bf16_part(w - hi)
    return hi, mid, w - hi - mid


def _key_bias_columns(key_weights, seq):
    w = key_weights[:, None] * jnp.arange(seq, dtype=F32)[None, :]
    cols = jnp.stack(_split3(w), axis=-1)
    return jnp.pad(cols, ((0, 0), (0, 0), (0, HEAD_DIM - cols.shape[-1]))).astype(BF16)


def _stage_kv(k_ref, v_ref, kb_ref, vt_ref):
    blocks = []
    for n in range(k_ref.shape[0] // MOBA_BLOCK):
        rows = slice(n * MOBA_BLOCK, (n + 1) * MOBA_BLOCK)
        kblk = k_ref[rows, :]
        kb_ref[rows, :] = kblk.astype(BF16)
        vt_ref[:, rows] = jnp.transpose(v_ref[rows, :]).astype(BF16)
        blocks.append(kblk)
    return blocks


def _with_bias_rows(qt):
    row = lax.broadcasted_iota(jnp.int32, qt.shape, 0)
    return jnp.concatenate([qt, jnp.where(row < 3, 1.0, 0.0).astype(BF16)], axis=0)


def _attend_block(i, qt, kb_ref, kbias_ref, vt_ref, scale2, causal, offsets, emit):
    nk = (i + 1) * MOBA_BLOCK
    raw = _dot_row_slabs(jnp.concatenate([kb_ref[0:nk, :], kbias_ref[0:nk, :]], axis=1), _with_bias_rows(qt),
                         2 if i else 1)
    yield
    ts, m = [], None
    for n in range(i + 1):
        t = raw[n * MOBA_BLOCK:(n + 1) * MOBA_BLOCK, :]
        if scale2 != 1.0:
            t = t * scale2
        if n == i:
            t = t + causal
        mb = jnp.max(t, axis=0, keepdims=True)
        if n < i and offsets:
            mb = mb + offsets[n]
        m = mb if m is None else jnp.maximum(m, mb)
        ts.append(t)
    yield
    ps, l = [], None
    for n in range(i + 1):
        p = jnp.exp2(ts[n] - (m - offsets[n] if n < i and offsets else m))
        ps.append(p.astype(BF16))
        pl_sum = jnp.sum(p, axis=0, keepdims=True)
        l = pl_sum if l is None else l + pl_sum
    emit(_dot(vt_ref[:, 0:nk], jnp.concatenate(ps, axis=0)) / l)


def _first_stage(stream):
    next(stream)
    return
    yield


def _interleave(*streams):
    streams = [s for s in streams if s is not None]
    while streams:
        for s in list(streams):
            try:
                next(s)
            except StopIteration:
                streams.remove(s)


def _moba_prompt_blocks(q_ref, k_ref, v_ref, kbias_ref, o_ref, kb_ref, vt_ref):
    nb = q_ref.shape[0] // MOBA_BLOCK
    kblocks = _stage_kv(k_ref, v_ref, kb_ref, vt_ref)
    km = jnp.concatenate([jnp.mean(kb, axis=0, keepdims=True) for kb in kblocks], axis=0)
    km_hi = km.astype(BF16)
    km_lo = (km - km_hi.astype(F32)).astype(BF16)

    def block(i):
        blk = lax.broadcasted_iota(jnp.int32, (nb, MOBA_BLOCK), 0)
        rows = slice(i * MOBA_BLOCK, (i + 1) * MOBA_BLOCK)
        qt = jnp.transpose(q_ref[rows, :]).astype(BF16)
        offsets = []
        if i > 0:
            gate = _dot(km_hi, qt) + _dot(km_lo, qt)
            cnt = jnp.zeros(gate.shape, F32)
            for m in range(i):
                gm = gate[m:m + 1, :]
                beats = (gm > gate) | ((gm == gate) & (m < blk))
                cnt = cnt + jnp.where(beats, 1.0, 0.0)
            selb = jnp.where((blk < i) & (cnt < MOBA_TOPK), 0.0, NEG_INF)
            offsets = [selb[n:n + 1, :] for n in range(i)]

        def emit(o):
            o_ref[rows, :] = jnp.transpose(o).astype(o_ref.dtype)

        yield from _attend_block(i, qt, kb_ref, kbias_ref, vt_ref, MOBA_SCALE * LOG2_E, _causal_bias(1),
                                 offsets, emit)

    return block


def _lambda(lam_refs, lam_init):
    lq1_ref, lk1_ref, lq2_ref, lk2_ref = lam_refs
    a = jnp.sum(lq1_ref[...] * lk1_ref[...], axis=-1, keepdims=True)
    b = jnp.sum(lq2_ref[...] * lk2_ref[...], axis=-1, keepdims=True)
    return jnp.exp(a) - jnp.exp(b) + lam_init


def _diff_prompt_blocks(lam, lam_init, gsub_ref, q_ref, k_ref, v_ref, kbias_ref, o_ref, kb_ref, vt_ref):
    qk_dim = HEAD_DIM // 2
    _stage_kv(k_ref, v_ref, kb_ref, vt_ref)

    def block(i):
        row = lax.broadcasted_iota(jnp.int32, (HEAD_DIM, MOBA_BLOCK), 0)
        rows = slice(i * MOBA_BLOCK, (i + 1) * MOBA_BLOCK)
        qt = jnp.transpose(q_ref[rows, :]) * (DIFF_SCALE * LOG2_E)
        qt2 = jnp.concatenate([jnp.where(row < qk_dim, qt, 0.0), jnp.where(row >= qk_dim, qt, 0.0)],
                              axis=1).astype(BF16)

        def emit(on):
            o = jnp.transpose(on[:, :MOBA_BLOCK] - lam * on[:, MOBA_BLOCK:])
            o_ref[rows, :] = (_rms(o, gsub_ref[...]) * (1.0 - lam_init)).astype(o_ref.dtype)

        yield from _attend_block(i, qt2, kb_ref, kbias_ref, vt_ref, 1.0, _causal_bias(2), [], emit)

    return block


class _PageRing:
    def __init__(self, pt_ref, kpool, vpool, ring, sem, layer, n_heads):
        self.pt_ref, self.kpool, self.vpool, self.ring, self.sem = pt_ref, kpool, vpool, ring, sem
        self.layer, self.n_heads = layer, n_heads
        self.page_rows = kpool.shape[2]
        self.keys = self.page_rows // n_heads
        self.steps = pt_ref.shape[1] // PAGES_PER_STEP
        self.ahead = RING_SLOTS - 1
        assert self.steps % RING_SLOTS == 0 and self.ahead < self.steps
        self.seq = pl.program_id(0) // 2
        self.n_seq = pl.num_programs(0) // 2

    def _copy(self, pool, page_idx, slot, gi):
        return pltpu.make_async_copy(pool.at[self.layer, page_idx],
                                     self.ring.at[slot, pl.ds(gi * self.page_rows, self.page_rows)],
                                     self.sem.at[slot])

    def _start(self, seq, sweep, step):
        pool = self.vpool if sweep else self.kpool
        for gi in range(PAGES_PER_STEP):
            self._copy(pool, self.pt_ref[seq, step * PAGES_PER_STEP + gi], step % RING_SLOTS, gi).start()

    def prologue(self):
        for step in range(self.ahead):
            self._start(0, 0, step)

    def advance(self, sweep, step):
        later = sweep * self.steps + step + self.ahead
        if later < 2 * self.steps:
            self._start(self.seq, later // self.steps, later % self.steps)
        else:
            @pl.when(self.seq + 1 < self.n_seq)
            def _():
                self._start(self.seq + 1, 0, later - 2 * self.steps)
        for gi in range(PAGES_PER_STEP):
            self._copy(self.kpool, 0, step % RING_SLOTS, gi).wait()

    def pages(self, step):
        slot = step % RING_SLOTS
        return jnp.concatenate([
            jnp.concatenate(
                [self.ring[slot, pl.ds(gi * self.page_rows + h, self.keys, stride=self.n_heads), :].astype(BF16)
                 for h in range(self.n_heads)], axis=1)
            for gi in range(PAGES_PER_STEP)], axis=0)


def _block_diag_q(q, groups, qbd_ref):
    dec, width = q.shape
    gd = width // groups
    qt = jnp.transpose(_pad_rows(q, LANES))
    for gi in range(groups):
        rows = slice(gi * gd, (gi + 1) * gd)
        blk = qt[rows, :]
        if gi:
            blk = pltpu.roll(blk, gi * dec, axis=1)
        qbd_ref[rows, :] = blk.astype(BF16)


class _SampleScores:
    def __init__(self, s_ref, stat_ref, l_ref, acc_ref, qbd_ref, lane_slope_ref, scale, past_len, dec):
        self.s_ref, self.stat_ref, self.l_ref, self.acc_ref, self.qbd_ref = s_ref, stat_ref, l_ref, acc_ref, qbd_ref
        self.scale, self.past_len = scale, past_len
        self.n_past_blk = past_len // MOBA_BLOCK
        self.lane_slope = lane_slope_ref[...]
        self.q_pos = past_len + lax.broadcasted_iota(jnp.int32, (1, LANES), 1) % dec

    def _rows(self, blk):
        return slice(blk * MOBA_BLOCK, (blk + 1) * MOBA_BLOCK)

    def _store(self, blk, logits):
        self.stat_ref[blk:blk + 1, :] = jnp.max(logits, axis=0, keepdims=True)
        self.s_ref[self._rows(blk), :] = logits

    def store_cached(self, blk, raw):
        in_block = lax.broadcasted_iota(jnp.int32, (MOBA_BLOCK, LANES), 0).astype(F32)
        first = -self.lane_slope * (self.q_pos - blk * MOBA_BLOCK).astype(F32)
        scaled = raw if self.scale == 1.0 else raw * self.scale
        self._store(blk, scaled + self.lane_slope * in_block + first)

    def store_new(self, knew):
        raw = _dot_row_slabs(_pad_rows(knew, MOBA_BLOCK).astype(BF16), self.qbd_ref[...])
        k_pos = self.past_len + lax.broadcasted_iota(jnp.int32, (MOBA_BLOCK, 1), 0)
        dist = self.q_pos - k_pos
        scaled = raw if self.scale == 1.0 else raw * self.scale
        self._store(self.n_past_blk, jnp.where(dist >= 0, scaled - self.lane_slope * dist.astype(F32), NEG_INF))

    def finish(self, selb=None):
        n = self.n_past_blk
        past = self.stat_ref[0:n, :] if selb is None else self.stat_ref[0:n, :] + selb
        m = jnp.maximum(jnp.max(past, axis=0, keepdims=True), self.stat_ref[n:n + 1, :])
        self.stat_ref[0:n, :] = jnp.broadcast_to(-m, past.shape) if selb is None else selb - m
        self.stat_ref[n:n + 1, :] = -m

    def pv(self, blocks, values, first):
        vb = values()
        yield
        l = jnp.zeros((1, LANES), F32) if first else self.l_ref[0:1, :]
        ps = []
        for blk in blocks:
            p = jnp.exp(self.s_ref[self._rows(blk), :] + self.stat_ref[blk:blk + 1, :])
            l = l + jnp.sum(p, axis=0, keepdims=True)
            ps.append(p)
        self.l_ref[0:1, :] = l
        pt = jnp.transpose(jnp.concatenate(ps, axis=0)).astype(BF16)
        yield
        part = _dot(pt, vb)
        self.acc_ref[...] = part if first else self.acc_ref[...] + part

    def row_scale(self):
        return jnp.transpose(jnp.broadcast_to(1.0 / self.l_ref[0:1, :], (LANES, LANES)))


def _sweeps(ring, prompt_block, n_prompt_blocks, k_step, k_done, v_step, v_done):
    g = pl.program_id(0)

    @pl.when(g == 0)
    def _():
        ring.prologue()

    for sweep, step_fn, done_fn in ((0, k_step, k_done), (1, v_step, v_done)):
        @pl.when(g % 2 == sweep)
        def _(sweep=sweep, step_fn=step_fn, done_fn=done_fn):
            blocks = [prompt_block(i) for i in range(n_prompt_blocks)]
            ahead = sweep == 1
            if ahead:
                next(blocks[0])
            for r in range(max(ring.steps, n_prompt_blocks)):
                if r < ring.steps:
                    ring.advance(sweep, r)
                _interleave(step_fn(r) if r < ring.steps else None,
                            blocks[r] if r < n_prompt_blocks else None,
                            _first_stage(blocks[r + 1]) if ahead and r + 1 < n_prompt_blocks else None)
            done_fn()


def _moba_kernel(pt_ref, lane_slope_ref, q_ref, k_ref, v_ref, kbias_ref, qs_ref, knew_ref, vnew_ref,
                 kpool, vpool, o_ref, os_ref, kb_ref, vt_ref, s_ref, acc_ref, l_ref, qbd_ref, ring_ref, sem,
                 stat_ref, gate_ref, *, layer, n_heads):
    ring = _PageRing(pt_ref, kpool, vpool, ring_ref, sem, layer, n_heads)
    past_len = ring.steps * PAGES_PER_STEP * ring.keys
    blocks_per_step = PAGES_PER_STEP * ring.keys // MOBA_BLOCK
    n_past_blk = past_len // MOBA_BLOCK
    dec = knew_ref.shape[0]
    prompt_block = _moba_prompt_blocks(q_ref, k_ref, v_ref, kbias_ref, o_ref, kb_ref, vt_ref)
    scores = _SampleScores(s_ref, stat_ref, l_ref, acc_ref, qbd_ref, lane_slope_ref, MOBA_SCALE, past_len, dec)

    def k_step(r):
        if r == 0:
            _block_diag_q(qs_ref[...], n_heads, qbd_ref)
        pages = ring.pages(r)
        yield
        sc = _dot_row_slabs(pages, qbd_ref[...])
        yield
        for b in range(blocks_per_step):
            blk = r * blocks_per_step + b
            raw = sc[b * MOBA_BLOCK:(b + 1) * MOBA_BLOCK, :]
            gate_ref[blk:blk + 1, :] = jnp.sum(raw, axis=0, keepdims=True) * (1.0 / MOBA_BLOCK)
            scores.store_cached(blk, raw)

    def k_done():
        scores.store_new(knew_ref[...])
        gate = gate_ref[0:n_past_blk, :]
        blk = lax.broadcasted_iota(jnp.int32, gate.shape, 0)
        cnt = jnp.zeros(gate.shape, F32)
        for m in range(n_past_blk):
            gm = gate[m:m + 1, :]
            beats = (gm > gate) | ((gm == gate) & (m < blk))
            cnt = cnt + jnp.where(beats, 1.0, 0.0)
        scores.finish(jnp.where(cnt < MOBA_TOPK, 0.0, NEG_INF))

    def v_step(r):
        return scores.pv(range(r * blocks_per_step, (r + 1) * blocks_per_step), lambda: ring.pages(r), r == 0)

    def v_done():
        _interleave(scores.pv([n_past_blk], lambda: _pad_rows(vnew_ref[...], MOBA_BLOCK).astype(BF16), False))
        acc, inv_l = acc_ref[...], scores.row_scale()
        for h in range(n_heads):
            cols, lanes = slice(h * HEAD_DIM, (h + 1) * HEAD_DIM), slice(h * dec, (h + 1) * dec)
            os_ref[:, cols] = acc[lanes, cols] * inv_l[lanes, :]

    _sweeps(ring, prompt_block, q_ref.shape[0] // MOBA_BLOCK, k_step, k_done, v_step, v_done)


def _diff_kernel(pt_ref, lane_slope_ref, lq1_ref, lk1_ref, lq2_ref, lk2_ref, gsub_ref,
                 q_ref, k_ref, v_ref, kbias_ref, qs_ref, knew_ref, vnew_ref, kpool, vpool, o_ref, os_ref,
                 kb_ref, vt_ref, s_ref, acc_ref, l_ref, qbd_ref, ring_ref, sem, stat_ref,
                 *, layer, n_heads, lam_init):
    ring = _PageRing(pt_ref, kpool, vpool, ring_ref, sem, layer, n_heads)
    past_len = ring.steps * PAGES_PER_STEP * ring.keys
    blocks_per_step = PAGES_PER_STEP * ring.keys // MOBA_BLOCK
    dec = knew_ref.shape[0]
    lam = _lambda((lq1_ref, lk1_ref, lq2_ref, lk2_ref), lam_init)
    prompt_block = _diff_prompt_blocks(lam, lam_init, gsub_ref, q_ref, k_ref, v_ref, kbias_ref, o_ref, kb_ref,
                                       vt_ref)
    scores = _SampleScores(s_ref, stat_ref, l_ref, acc_ref, qbd_ref, lane_slope_ref, 1.0, past_len, dec)

    def k_step(r):
        if r == 0:
            _block_diag_q(qs_ref[...] * DIFF_SCALE, 2 * n_heads, qbd_ref)
        pages = ring.pages(r)
        yield
        sc = _dot_row_slabs(pages, qbd_ref[...])
        yield
        for b in range(blocks_per_step):
            scores.store_cached(r * blocks_per_step + b, sc[b * MOBA_BLOCK:(b + 1) * MOBA_BLOCK, :])

    def k_done():
        scores.store_new(knew_ref[...])
        scores.finish()

    def v_step(r):
        return scores.pv(range(r * blocks_per_step, (r + 1) * blocks_per_step), lambda: ring.pages(r), r == 0)

    def v_done():
        _interleave(scores.pv([scores.n_past_blk], lambda: _pad_rows(vnew_ref[...], MOBA_BLOCK).astype(BF16),
                              False))
        acc, inv_l = acc_ref[...], scores.row_scale()
        for h in range(n_heads):
            cols = slice(h * HEAD_DIM, (h + 1) * HEAD_DIM)
            map1, map2 = slice(2 * h * dec, (2 * h + 1) * dec), slice((2 * h + 1) * dec, (2 * h + 2) * dec)
            o = acc[map1, cols] * inv_l[map1, :] - lam * (acc[map2, cols] * inv_l[map2, :])
            os_ref[:, cols] = _rms(o, gsub_ref[...]) * (1.0 - lam_init)

    _sweeps(ring, prompt_block, q_ref.shape[0] // MOBA_BLOCK, k_step, k_done, v_step, v_done)


def _attention(kern, name, key_weights, small_inputs, q, k, v, qs, knew, vnew, pool_k, pool_v, page_table,
               n_heads, with_gate):
    b, t, width = q.shape
    n_seq, n_pages = page_table.shape
    page_rows, hd = pool_k.shape[2], pool_k.shape[3]
    dec = qs.shape[1]
    past_len = n_pages * (page_rows // n_heads)
    assert b * n_heads == 2 * n_seq, "one prompt (batch, head) and one sample sweep per grid step"
    prompt = pl.BlockSpec((None, t, HEAD_DIM), lambda g, pt: (g // n_heads, 0, g % n_heads))
    per_seq = pl.BlockSpec((None, dec, width), lambda g, pt: (g // 2, 0, 0))
    small = lambda a: pl.BlockSpec(a.shape, lambda g, pt: (0,) * a.ndim)
    per_head = pl.BlockSpec((None, t, HEAD_DIM), lambda g, pt: (g % n_heads, 0, 0))
    hbm = pl.BlockSpec(memory_space=pl.ANY)
    scratch = [
        pltpu.VMEM((t, HEAD_DIM), BF16),
        pltpu.VMEM((HEAD_DIM, t), BF16),
        pltpu.VMEM((past_len + MOBA_BLOCK, LANES), F32),
        pltpu.VMEM((LANES, width), F32),
        pltpu.VMEM((8, LANES), F32),
        pltpu.VMEM((width, LANES), BF16),
        pltpu.VMEM((RING_SLOTS, PAGES_PER_STEP * page_rows, hd), pool_k.dtype),
        pltpu.SemaphoreType.DMA((RING_SLOTS,)),
        pltpu.VMEM((past_len // MOBA_BLOCK + 8, LANES), F32),
    ]
    if with_gate:
        scratch.append(pltpu.VMEM((past_len // MOBA_BLOCK, LANES), F32))
    return pl.pallas_call(
        kern,
        grid_spec=pltpu.PrefetchScalarGridSpec(
            num_scalar_prefetch=1,
            grid=(b * n_heads,),
            in_specs=[small(a) for a in small_inputs]
                     + [prompt, prompt, prompt, per_head, per_seq, per_seq, per_seq, hbm, hbm],
            out_specs=[prompt, per_seq],
            scratch_shapes=scratch,
        ),
        out_shape=[jax.ShapeDtypeStruct((b, t, width), BF16), jax.ShapeDtypeStruct((n_seq, dec, width), F32)],
        compiler_params=pltpu.CompilerParams(
            dimension_semantics=("arbitrary",), vmem_limit_bytes=VMEM_LIMIT),
        name=name,
    )(page_table, *small_inputs, q, k, v, _key_bias_columns(key_weights, t), qs, knew, vnew, pool_k, pool_v)


def _oproj_kernel(om_ref, od_ref, x_ref, w_ref, gpost_ref, gpre_ref, x1_ref, h2_ref):
    slab = x_ref.shape[0] // OPROJ_SLABS
    for s in range(OPROJ_SLABS):
        rows = slice(s * slab, (s + 1) * slab)
        o = jnp.concatenate([om_ref[rows, :].astype(BF16), od_ref[rows, :].astype(BF16)], axis=1)
        x1 = x_ref[rows, :] + _rms(_dot(o, w_ref[...]), gpost_ref[...])
        x1_ref[rows, :] = x1
        h2_ref[rows, :] = _rms(x1, gpre_ref[...]).astype(BF16)


def _oproj(om, od, x, w, gpost, gpre, tm):
    m, d = x.shape
    row = lambda width: pl.BlockSpec((tm, width), lambda i: (i, 0))
    const = lambda a: pl.BlockSpec(a.shape, lambda i: (0, 0))
    return pl.pallas_call(
        _oproj_kernel,
        grid=(m // tm,),
        in_specs=[row(om.shape[1]), row(od.shape[1]), row(d), const(w), const(gpost), const(gpre)],
        out_specs=[row(d), row(d)],
        out_shape=[jax.ShapeDtypeStruct((m, d), F32), jax.ShapeDtypeStruct((m, d), BF16)],
        compiler_params=pltpu.CompilerParams(
            dimension_semantics=("arbitrary",), vmem_limit_bytes=VMEM_LIMIT),
        name="oproj",
    )(om, od, x, w, gpost, gpre)


def _ffn_kernel(h_ref, x_ref, wg_ref, wu_ref, wd_ref, g_ref, o_ref, acc_ref):
    j = pl.program_id(1)

    @pl.when(j == 0)
    def _():
        acc_ref[...] = jnp.zeros(acc_ref.shape, F32)

    h = h_ref[...]
    gate = _dot(h, wg_ref[...])
    up = _dot(h, wu_ref[...])
    a = (gate * (1.0 / (1.0 + jnp.exp(-gate))) * up).astype(BF16)
    acc_ref[...] += _dot(a, wd_ref[...])

    @pl.when(j == pl.num_programs(1) - 1)
    def _():
        o_ref[...] = x_ref[...] + _rms(acc_ref[...], g_ref[...])


def _ffn(h, x, wg, wu, wd, g, tm, tf):
    m, d = x.shape
    f = wg.shape[1]
    return pl.pallas_call(
        _ffn_kernel,
        grid=(m // tm, f // tf),
        in_specs=[
            pl.BlockSpec((tm, d), lambda i, j: (i, 0)),
            pl.BlockSpec((tm, d), lambda i, j: (i, 0)),
            pl.BlockSpec((d, tf), lambda i, j: (0, j)),
            pl.BlockSpec((d, tf), lambda i, j: (0, j)),
            pl.BlockSpec((tf, d), lambda i, j: (j, 0)),
            pl.BlockSpec((1, d), lambda i, j: (0, 0)),
        ],
        out_specs=pl.BlockSpec((tm, d), lambda i, j: (i, 0)),
        out_shape=jax.ShapeDtypeStruct((m, d), F32),
        scratch_shapes=[pltpu.VMEM((tm, d), F32)],
        compiler_params=pltpu.CompilerParams(
            dimension_semantics=("arbitrary", "arbitrary"), vmem_limit_bytes=VMEM_LIMIT),
        name="ffn",
    )(h, x, wg, wu, wd, g)


def kernel(x_prompt, x_sample, cache_moba_k, cache_moba_v, cache_diff_k, cache_diff_v, page_table, g_pre_attn, w_in, lambda_q1, lambda_k1, lambda_q2, lambda_k2, g_subln, w_o, g_post_attn, g_pre_ffn, w_gate, w_up, w_down, g_post_ffn):
    batch, seq, d_model = x_prompt.shape
    dec_batch, dec_seq, _ = x_sample.shape
    depth, n_pool, page, h_moba, _ = cache_moba_k.shape
    h_diff = cache_diff_k.shape[3]
    n_heads = h_moba + h_diff
    wm, wd_ = h_moba * HEAD_DIM, h_diff * HEAD_DIM
    past_len = page_table.shape[1] * page
    assert seq % MOBA_BLOCK == 0 and past_len % MOBA_BLOCK == 0
    assert dec_seq <= MOBA_BLOCK and (page * PAGES_PER_STEP) % MOBA_BLOCK == 0
    assert wm == GROUP_WIDTH and wd_ == GROUP_WIDTH and h_moba * dec_seq <= LANES and 2 * h_diff * dec_seq <= LANES
    assert cache_diff_k.shape[4] == HEAD_DIM and cache_diff_v.shape[4] == HEAD_DIM

    slopes = _alibi_slopes(n_heads)
    s_moba, s_diff = slopes[0::2], slopes[1::2]
    pad_lanes = lambda v: jnp.pad(v, (0, LANES - v.shape[0])).reshape(1, LANES)
    lane_slope_m = pad_lanes(jnp.repeat(s_moba, dec_seq))
    lane_slope_d = pad_lanes(jnp.repeat(s_diff, 2 * dec_seq))
    pools = [c.reshape(depth, n_pool, -1, HEAD_DIM) for c in (cache_moba_k, cache_moba_v, cache_diff_k, cache_diff_v)]

    xp = x_prompt.reshape(batch * seq, d_model)
    xs = x_sample.reshape(dec_batch * dec_seq, d_model)
    tm_p, tm_s = 512, dec_batch * dec_seq
    kv_p = [[] for _ in range(4)]
    kv_s = [[] for _ in range(4)]
    for l in range(depth):
        lam_init = _lambda_init(l)
        row = lambda a: a[l].reshape(1, -1)
        lam_params = [row(lambda_q1), row(lambda_k1), row(lambda_q2), row(lambda_k2)]
        w_in_b, w_o_b = w_in[l].astype(BF16), w_o[l].astype(BF16)
        wg_b, wu_b, wd_b = w_gate[l].astype(BF16), w_up[l].astype(BF16), w_down[l].astype(BF16)

        proj_p = _proj(xp, row(g_pre_attn), w_in_b, 1024, 512)
        proj_s = _proj(xs, row(g_pre_attn), w_in_b, tm_s, GROUP_WIDTH)
        seq3 = lambda a: a.reshape(batch, seq, -1)
        dec3 = lambda a: a.reshape(dec_batch, dec_seq, -1)
        mq, mk, mv, dq, dk, dv = [seq3(a) for a in proj_p]
        mqs, mks, mvs, dqs, dks, dvs = [dec3(a) for a in proj_s]
        om_p, om_s = _attention(
            functools.partial(_moba_kernel, layer=l, n_heads=h_moba), "moba_attention", s_moba * (1.0 / MOBA_SCALE),
            [lane_slope_m], mq, mk, mv, mqs, mks, mvs, pools[0], pools[1], page_table, h_moba, True)
        od_p, od_s = _attention(
            functools.partial(_diff_kernel, layer=l, n_heads=h_diff, lam_init=lam_init), "diff_attention",
            s_diff * LOG2_E,
            [lane_slope_d] + lam_params + [row(g_subln)], dq, dk, dv, dqs, dks, dvs, pools[2], pools[3],
            page_table, h_diff, False)

        x1, h2 = _oproj(om_p.reshape(batch * seq, wm), od_p.reshape(batch * seq, wd_), xp, w_o_b,
                        row(g_post_attn), row(g_pre_ffn), tm_p)
        xp = _ffn(h2, x1, wg_b, wu_b, wd_b, row(g_post_ffn), tm_p, 512)
        x1, h2 = _oproj(om_s.reshape(tm_s, wm), od_s.reshape(tm_s, wd_), xs, w_o_b,
                        row(g_post_attn), row(g_pre_ffn), tm_s)
        xs = _ffn(h2, x1, wg_b, wu_b, wd_b, row(g_post_ffn), tm_s, 512)
        for dst, a in zip(kv_p, (mk, mv, dk, dv)):
            dst.append(a.reshape(batch, seq, -1, HEAD_DIM))
        for dst, a in zip(kv_s, (mks, mvs, dks, dvs)):
            dst.append(a.reshape(dec_batch, dec_seq, -1, HEAD_DIM))

    return (xp.reshape(batch, seq, d_model), xs.reshape(dec_batch, dec_seq, d_model),
            *[jnp.stack(t) for t in kv_p], *[jnp.stack(t) for t in kv_s])
```

```python
import functools
import math

import jax
import jax.numpy as jnp
from jax import lax
from jax.experimental import pallas as pl
from jax.experimental.pallas import tpu as pltpu

F32 = jnp.float32
BF16 = jnp.bfloat16
NEG_INF = float("-inf")
LOG2_E = math.log2(math.e)

HEAD_DIM = 128
GROUP_WIDTH = 1024
LANES = 128
MXU_COLUMNS = 256
MOBA_BLOCK = 256
MOBA_TOPK = 3
MOBA_SCALE = HEAD_DIM ** -0.5
DIFF_SCALE = (HEAD_DIM // 2) ** -0.5
RMS_EPS = 1e-6
PAGES_PER_STEP = 4
RING_SLOTS = 8
OPROJ_SLABS = 2
PROJ_SLABS = 4
VMEM_LIMIT = 56 * 1024 * 1024


def _alibi_slopes(n_heads):
    return 2.0 ** (-8.0 * jnp.arange(1, n_heads + 1, dtype=F32) / n_heads)


def _lambda_init(layer):
    return 0.8 - 0.6 * math.exp(-0.3 * layer)


def _rms(x, g):
    return (x * lax.rsqrt(jnp.mean(x * x, axis=-1, keepdims=True) + RMS_EPS)) * g


def _dot(a, b):
    return jnp.dot(a, b, preferred_element_type=F32)


def _dot_row_slabs(a, b, slabs=2):
    if b.shape[1] > MXU_COLUMNS or slabs == 1:
        return _dot(a, b)
    rows = a.shape[0] // slabs
    return jnp.concatenate([_dot(a[i * rows:(i + 1) * rows, :], b) for i in range(slabs)], axis=0)


def _pad_rows(a, rows):
    return jnp.concatenate([a, jnp.zeros((rows - a.shape[0], a.shape[1]), a.dtype)], axis=0)


def _proj_kernel(x_ref, g_ref, w_ref, *refs):
    outs, h_ref = refs[:-1], refs[-1]
    j = pl.program_id(1)

    @pl.when(j == 0)
    def _():
        slab = x_ref.shape[0] // PROJ_SLABS
        for s in range(PROJ_SLABS):
            rows = slice(s * slab, (s + 1) * slab)
            h = _rms(x_ref[rows, :], g_ref[...]).astype(BF16)
            h_ref[rows, :] = h
            outs[0][rows, :] = _dot(h, w_ref[...])

    per_out = GROUP_WIDTH // outs[0].shape[1]
    for k, o_ref in enumerate(outs):
        @pl.when((j >= max(k * per_out, 1)) & (j < (k + 1) * per_out))
        def _(o_ref=o_ref):
            o_ref[...] = _dot(h_ref[...], w_ref[...])


def _proj(x, g, w, tm, tn):
    m, d = x.shape
    n_out = w.shape[1] // GROUP_WIDTH
    per_out = GROUP_WIDTH // tn
    last_i = m // tm - 1

    def out_map(k):
        def index(i, j):
            done = j >= (k + 1) * per_out
            move = done & (i < last_i)
            col = jnp.where(done, jnp.where(move, 0, per_out - 1), jnp.clip(j - k * per_out, 0, per_out - 1))
            return jnp.where(move, i + 1, i), col
        return index

    return pl.pallas_call(
        _proj_kernel,
        grid=(m // tm, n_out * per_out),
        in_specs=[
            pl.BlockSpec((tm, d), lambda i, j: (jnp.minimum(i + jnp.minimum(j, 1), last_i), 0)),
            pl.BlockSpec((1, d), lambda i, j: (0, 0)),
            pl.BlockSpec((d, tn), lambda i, j: (0, j)),
        ],
        out_specs=[pl.BlockSpec((tm, tn), out_map(k)) for k in range(n_out)],
        out_shape=[jax.ShapeDtypeStruct((m, GROUP_WIDTH), F32) for _ in range(n_out)],
        scratch_shapes=[pltpu.VMEM((tm, d), BF16)],
        compiler_params=pltpu.CompilerParams(
            dimension_semantics=("arbitrary", "arbitrary"), vmem_limit_bytes=VMEM_LIMIT),
        name="proj",
    )(x, g, w)


def _causal_bias(maps):
    t = lax.broadcasted_iota(jnp.int32, (MOBA_BLOCK, MOBA_BLOCK), 1)
    j = lax.broadcasted_iota(jnp.int32, (MOBA_BLOCK, MOBA_BLOCK), 0)
    c = jnp.where(j <= t, 0.0, NEG_INF)
    return c if maps == 1 else jnp.concatenate([c] * maps, axis=1)


def _bf16_part(w):
    bits = lax.bitcast_convert_type(w, jnp.uint32) & jnp.uint32(0xFFFF0000)
    return lax.bitcast_convert_type(bits, F32)


def _split3(w):
    hi = _bf16_part(w)
    mid = _bf16_part(w - hi)
    return hi, mid, w - hi - mid


def _key_bias_columns(key_weights, seq):
    w = key_weights[:, None] * jnp.arange(seq, dtype=F32)[None, :]
    cols = jnp.stack(_split3(w), axis=-1)
    return jnp.pad(cols, ((0, 0), (0, 0), (0, HEAD_DIM - cols.shape[-1]))).astype(BF16)


def _stage_kv(k_ref, v_ref, kb_ref, vt_ref):
    blocks = []
    for n in range(k_ref.shape[0] // MOBA_BLOCK):
        rows = slice(n * MOBA_BLOCK, (n + 1) * MOBA_BLOCK)
        kblk = k_ref[rows, :]
        kb_ref[rows, :] = kblk.astype(BF16)
        vt_ref[:, rows] = jnp.transpose(v_ref[rows, :]).astype(BF16)
        blocks.append(kblk)
    return blocks


def _with_bias_rows(qt):
    row = lax.broadcasted_iota(jnp.int32, qt.shape, 0)
    return jnp.concatenate([qt, jnp.where(row < 3, 1.0, 0.0).astype(BF16)], axis=0)


def _attend_block(i, qt, kb_ref, kbias_ref, vt_ref, scale2, causal, offsets, emit):
    nk = (i + 1) * MOBA_BLOCK
    raw = _dot_row_slabs(jnp.concatenate([kb_ref[0:nk, :], kbias_ref[0:nk, :]], axis=1), _with_bias_rows(qt),
                         2 if i else 1)
    yield
    ts, m = [], None
    for n in range(i + 1):
        t = raw[n * MOBA_BLOCK:(n + 1) * MOBA_BLOCK, :]
        if scale2 != 1.0:
            t = t * scale2
        if n == i:
            t = t + causal
        mb = jnp.max(t, axis=0, keepdims=True)
        if n < i and offsets:
            mb = mb + offsets[n]
        m = mb if m is None else jnp.maximum(m, mb)
        ts.append(t)
    yield
    ps, l = [], None
    for n in range(i + 1):
        p = jnp.exp2(ts[n] - (m - offsets[n] if n < i and offsets else m))
        ps.append(p.astype(BF16))
        pl_sum = jnp.sum(p, axis=0, keepdims=True)
        l = pl_sum if l is None else l + pl_sum
    emit(_dot(vt_ref[:, 0:nk], jnp.concatenate(ps, axis=0)) / l)


def _take(stream, n):
    for _ in range(n):
        next(stream)
        yield


def _interleave(*streams):
    streams = [s for s in streams if s is not None]
    while streams:
        for s in list(streams):
            try:
                next(s)
            except StopIteration:
                streams.remove(s)


def _moba_prompt_blocks(q_ref, k_ref, v_ref, kbias_ref, o_ref, kb_ref, vt_ref):
    nb = q_ref.shape[0] // MOBA_BLOCK
    kblocks = _stage_kv(k_ref, v_ref, kb_ref, vt_ref)
    km = jnp.concatenate([jnp.mean(kb, axis=0, keepdims=True) for kb in kblocks], axis=0)
    km_hi = km.astype(BF16)
    km_lo = (km - km_hi.astype(F32)).astype(BF16)

    def block(i):
        blk = lax.broadcasted_iota(jnp.int32, (nb, MOBA_BLOCK), 0)
        rows = slice(i * MOBA_BLOCK, (i + 1) * MOBA_BLOCK)
        qt = jnp.transpose(q_ref[rows, :]).astype(BF16)
        offsets = []
        if i > 0:
            gate = _dot(km_hi, qt) + _dot(km_lo, qt)
            cnt = jnp.zeros(gate.shape, F32)
            for m in range(i):
                gm = gate[m:m + 1, :]
                beats = (gm > gate) | ((gm == gate) & (m < blk))
                cnt = cnt + jnp.where(beats, 1.0, 0.0)
            selb = jnp.where((blk < i) & (cnt < MOBA_TOPK), 0.0, NEG_INF)
            offsets = [selb[n:n + 1, :] for n in range(i)]

        def emit(o):
            o_ref[rows, :] = jnp.transpose(o).astype(o_ref.dtype)

        yield from _attend_block(i, qt, kb_ref, kbias_ref, vt_ref, MOBA_SCALE * LOG2_E, _causal_bias(1),
                                 offsets, emit)

    return block


def _lambda(lam_refs, lam_init):
    lq1_ref, lk1_ref, lq2_ref, lk2_ref = lam_refs
    a = jnp.sum(lq1_ref[...] * lk1_ref[...], axis=-1, keepdims=True)
    b = jnp.sum(lq2_ref[...] * lk2_ref[...], axis=-1, keepdims=True)
    return jnp.exp(a) - jnp.exp(b) + lam_init


def _diff_prompt_blocks(lam, lam_init, gsub_ref, q_ref, k_ref, v_ref, kbias_ref, o_ref, kb_ref, vt_ref):
    qk_dim = HEAD_DIM // 2
    _stage_kv(k_ref, v_ref, kb_ref, vt_ref)

    def block(i):
        row = lax.broadcasted_iota(jnp.int32, (HEAD_DIM, MOBA_BLOCK), 0)
        rows = slice(i * MOBA_BLOCK, (i + 1) * MOBA_BLOCK)
        qt = jnp.transpose(q_ref[rows, :]) * (DIFF_SCALE * LOG2_E)
        qt2 = jnp.concatenate([jnp.where(row < qk_dim, qt, 0.0), jnp.where(row >= qk_dim, qt, 0.0)],
                              axis=1).astype(BF16)

        def emit(on):
            o = jnp.transpose(on[:, :MOBA_BLOCK] - lam * on[:, MOBA_BLOCK:])
            o_ref[rows, :] = (_rms(o, gsub_ref[...]) * (1.0 - lam_init)).astype(o_ref.dtype)

        yield from _attend_block(i, qt2, kb_ref, kbias_ref, vt_ref, 1.0, _causal_bias(2), [], emit)

    return block


class _PageRing:
    def __init__(self, pt_ref, kpool, vpool, ring, sem, layer, n_heads):
        self.pt_ref, self.kpool, self.vpool, self.ring, self.sem = pt_ref, kpool, vpool, ring, sem
        self.layer, self.n_heads = layer, n_heads
        self.page_rows = kpool.shape[2]
        self.keys = self.page_rows // n_heads
        self.steps = pt_ref.shape[1] // PAGES_PER_STEP
        self.ahead = RING_SLOTS - 1
        assert self.steps % RING_SLOTS == 0 and self.ahead < self.steps
        self.seq = pl.program_id(0) // 2
        self.n_seq = pl.num_programs(0) // 2

    def _copy(self, pool, page_idx, slot, gi):
        return pltpu.make_async_copy(pool.at[self.layer, page_idx],
                                     self.ring.at[slot, pl.ds(gi * self.page_rows, self.page_rows)],
                                     self.sem.at[slot])

    def _start(self, seq, sweep, step):
        pool = self.vpool if sweep else self.kpool
        for gi in range(PAGES_PER_STEP):
            self._copy(pool, self.pt_ref[seq, step * PAGES_PER_STEP + gi], step % RING_SLOTS, gi).start()

    def prologue(self):
        for step in range(self.ahead):
            self._start(0, 0, step)

    def advance(self, sweep, step):
        later = sweep * self.steps + step + self.ahead
        if later < 2 * self.steps:
            self._start(self.seq, later // self.steps, later % self.steps)
        else:
            @pl.when(self.seq + 1 < self.n_seq)
            def _():
                self._start(self.seq + 1, 0, later - 2 * self.steps)
        for gi in range(PAGES_PER_STEP):
            self._copy(self.kpool, 0, step % RING_SLOTS, gi).wait()

    def pages(self, step):
        slot = step % RING_SLOTS
        return jnp.concatenate([
            jnp.concatenate(
                [self.ring[slot, pl.ds(gi * self.page_rows + h, self.keys, stride=self.n_heads), :].astype(BF16)
                 for h in range(self.n_heads)], axis=1)
            for gi in range(PAGES_PER_STEP)], axis=0)


def _block_diag_q(q, groups, qbd_ref):
    dec, width = q.shape
    gd = width // groups
    qt = jnp.transpose(_pad_rows(q, LANES))
    for gi in range(groups):
        rows = slice(gi * gd, (gi + 1) * gd)
        blk = qt[rows, :]
        if gi:
            blk = pltpu.roll(blk, gi * dec, axis=1)
        qbd_ref[rows, :] = blk.astype(BF16)


class _SampleScores:
    def __init__(self, s_ref, stat_ref, l_ref, acc_ref, qbd_ref, lane_slope_ref, scale, past_len, dec):
        self.s_ref, self.stat_ref, self.l_ref, self.acc_ref, self.qbd_ref = s_ref, stat_ref, l_ref, acc_ref, qbd_ref
        self.scale, self.past_len = scale, past_len
        self.n_past_blk = past_len // MOBA_BLOCK
        self.lane_slope = lane_slope_ref[...]
        self.q_pos = past_len + lax.broadcasted_iota(jnp.int32, (1, LANES), 1) % dec

    def _rows(self, blk):
        return slice(blk * MOBA_BLOCK, (blk + 1) * MOBA_BLOCK)

    def _store(self, blk, logits):
        self.stat_ref[blk:blk + 1, :] = jnp.max(logits, axis=0, keepdims=True)
        self.s_ref[self._rows(blk), :] = logits

    def store_cached(self, blk, raw):
        in_block = lax.broadcasted_iota(jnp.int32, (MOBA_BLOCK, LANES), 0).astype(F32)
        first = -self.lane_slope * (self.q_pos - blk * MOBA_BLOCK).astype(F32)
        scaled = raw if self.scale == 1.0 else raw * self.scale
        self._store(blk, scaled + self.lane_slope * in_block + first)

    def store_new(self, knew):
        raw = _dot_row_slabs(_pad_rows(knew, MOBA_BLOCK).astype(BF16), self.qbd_ref[...])
        k_pos = self.past_len + lax.broadcasted_iota(jnp.int32, (MOBA_BLOCK, 1), 0)
        dist = self.q_pos - k_pos
        scaled = raw if self.scale == 1.0 else raw * self.scale
        self._store(self.n_past_blk, jnp.where(dist >= 0, scaled - self.lane_slope * dist.astype(F32), NEG_INF))

    def finish(self, selb=None):
        n = self.n_past_blk
        past = self.stat_ref[0:n, :] if selb is None else self.stat_ref[0:n, :] + selb
        m = jnp.maximum(jnp.max(past, axis=0, keepdims=True), self.stat_ref[n:n + 1, :])
        self.stat_ref[0:n, :] = jnp.broadcast_to(-m, past.shape) if selb is None else selb - m
        self.stat_ref[n:n + 1, :] = -m

    def pv(self, blocks, values, first):
        vb = values()
        yield
        l = jnp.zeros((1, LANES), F32) if first else self.l_ref[0:1, :]
        ps = []
        for blk in blocks:
            p = jnp.exp(self.s_ref[self._rows(blk), :] + self.stat_ref[blk:blk + 1, :])
            l = l + jnp.sum(p, axis=0, keepdims=True)
            ps.append(p)
        self.l_ref[0:1, :] = l
        pt = jnp.transpose(jnp.concatenate(ps, axis=0)).astype(BF16)
        yield
        part = _dot(pt, vb)
        self.acc_ref[...] = part if first else self.acc_ref[...] + part

    def row_scale(self):
        return jnp.transpose(jnp.broadcast_to(1.0 / self.l_ref[0:1, :], (LANES, LANES)))


def _sweeps(ring, prompt_block, n_prompt_blocks, k_step, k_done, v_step, v_done):
    g = pl.program_id(0)

    @pl.when(g == 0)
    def _():
        ring.prologue()

    for sweep, step_fn, done_fn in ((0, k_step, k_done), (1, v_step, v_done)):
        @pl.when(g % 2 == sweep)
        def _(sweep=sweep, step_fn=step_fn, done_fn=done_fn):
            blocks = [prompt_block(i) for i in range(n_prompt_blocks)]
            n_regions = max(ring.steps, n_prompt_blocks)
            per_block = n_regions // n_prompt_blocks
            assert per_block in (1, 2)
            ahead = sweep == 1
            if ahead:
                next(blocks[0])
            for r in range(n_regions):
                if r < ring.steps:
                    ring.advance(sweep, r)
                i, part = divmod(r, per_block)
                streams = [step_fn(r) if r < ring.steps else None]
                if i < n_prompt_blocks:
                    closing = part == per_block - 1
                    streams.append(blocks[i] if closing else _take(blocks[i], 1 if ahead else 2))
                    if ahead and closing and i + 1 < n_prompt_blocks:
                        streams.append(_take(blocks[i + 1], 1))
                _interleave(*streams)
            done_fn()


def _moba_kernel(pt_ref, lane_slope_ref, q_ref, k_ref, v_ref, kbias_ref, qs_ref, knew_ref, vnew_ref,
                 kpool, vpool, o_ref, os_ref, kb_ref, vt_ref, s_ref, acc_ref, l_ref, qbd_ref, ring_ref, sem,
                 stat_ref, gate_ref, *, layer, n_heads):
    ring = _PageRing(pt_ref, kpool, vpool, ring_ref, sem, layer, n_heads)
    past_len = ring.steps * PAGES_PER_STEP * ring.keys
    blocks_per_step = PAGES_PER_STEP * ring.keys // MOBA_BLOCK
    n_past_blk = past_len // MOBA_BLOCK
    dec = knew_ref.shape[0]
    prompt_block = _moba_prompt_blocks(q_ref, k_ref, v_ref, kbias_ref, o_ref, kb_ref, vt_ref)
    scores = _SampleScores(s_ref, stat_ref, l_ref, acc_ref, qbd_ref, lane_slope_ref, MOBA_SCALE, past_len, dec)

    def k_step(r):
        if r == 0:
            _block_diag_q(qs_ref[...], n_heads, qbd_ref)
        pages = ring.pages(r)
        yield
        sc = _dot_row_slabs(pages, qbd_ref[...])
        yield
        for b in range(blocks_per_step):
            blk = r * blocks_per_step + b
            raw = sc[b * MOBA_BLOCK:(b + 1) * MOBA_BLOCK, :]
            gate_ref[blk:blk + 1, :] = jnp.sum(raw, axis=0, keepdims=True) * (1.0 / MOBA_BLOCK)
            scores.store_cached(blk, raw)

    def k_done():
        scores.store_new(knew_ref[...])
        gate = gate_ref[0:n_past_blk, :]
        blk = lax.broadcasted_iota(jnp.int32, gate.shape, 0)
        cnt = jnp.zeros(gate.shape, F32)
        for m in range(n_past_blk):
            gm = gate[m:m + 1, :]
            beats = (gm > gate) | ((gm == gate) & (m < blk))
            cnt = cnt + jnp.where(beats, 1.0, 0.0)
        scores.finish(jnp.where(cnt < MOBA_TOPK, 0.0, NEG_INF))

    def v_step(r):
        return scores.pv(range(r * blocks_per_step, (r + 1) * blocks_per_step), lambda: ring.pages(r), r == 0)

    def v_done():
        _interleave(scores.pv([n_past_blk], lambda: _pad_rows(vnew_ref[...], MOBA_BLOCK).astype(BF16), False))
        acc, inv_l = acc_ref[...], scores.row_scale()
        for h in range(n_heads):
            cols, lanes = slice(h * HEAD_DIM, (h + 1) * HEAD_DIM), slice(h * dec, (h + 1) * dec)
            os_ref[:, cols] = acc[lanes, cols] * inv_l[lanes, :]

    _sweeps(ring, prompt_block, q_ref.shape[0] // MOBA_BLOCK, k_step, k_done, v_step, v_done)


def _diff_kernel(pt_ref, lane_slope_ref, lq1_ref, lk1_ref, lq2_ref, lk2_ref, gsub_ref,
                 q_ref, k_ref, v_ref, kbias_ref, qs_ref, knew_ref, vnew_ref, kpool, vpool, o_ref, os_ref,
                 kb_ref, vt_ref, s_ref, acc_ref, l_ref, qbd_ref, ring_ref, sem, stat_ref,
                 *, layer, n_heads, lam_init):
    ring = _PageRing(pt_ref, kpool, vpool, ring_ref, sem, layer, n_heads)
    past_len = ring.steps * PAGES_PER_STEP * ring.keys
    blocks_per_step = PAGES_PER_STEP * ring.keys // MOBA_BLOCK
    dec = knew_ref.shape[0]
    lam = _lambda((lq1_ref, lk1_ref, lq2_ref, lk2_ref), lam_init)
    prompt_block = _diff_prompt_blocks(lam, lam_init, gsub_ref, q_ref, k_ref, v_ref, kbias_ref, o_ref, kb_ref,
                                       vt_ref)
    scores = _SampleScores(s_ref, stat_ref, l_ref, acc_ref, qbd_ref, lane_slope_ref, 1.0, past_len, dec)

    def k_step(r):
        if r == 0:
            _block_diag_q(qs_ref[...] * DIFF_SCALE, 2 * n_heads, qbd_ref)
        pages = ring.pages(r)
        yield
        sc = _dot_row_slabs(pages, qbd_ref[...])
        yield
        for b in range(blocks_per_step):
            scores.store_cached(r * blocks_per_step + b, sc[b * MOBA_BLOCK:(b + 1) * MOBA_BLOCK, :])

    def k_done():
        scores.store_new(knew_ref[...])
        scores.finish()

    def v_step(r):
        return scores.pv(range(r * blocks_per_step, (r + 1) * blocks_per_step), lambda: ring.pages(r), r == 0)

    def v_done():
        _interleave(scores.pv([scores.n_past_blk], lambda: _pad_rows(vnew_ref[...], MOBA_BLOCK).astype(BF16),
                              False))
        acc, inv_l = acc_ref[...], scores.row_scale()
        for h in range(n_heads):
            cols = slice(h * HEAD_DIM, (h + 1) * HEAD_DIM)
            map1, map2 = slice(2 * h * dec, (2 * h + 1) * dec), slice((2 * h + 1) * dec, (2 * h + 2) * dec)
            o = acc[map1, cols] * inv_l[map1, :] - lam * (acc[map2, cols] * inv_l[map2, :])
            os_ref[:, cols] = _rms(o, gsub_ref[...]) * (1.0 - lam_init)

    _sweeps(ring, prompt_block, q_ref.shape[0] // MOBA_BLOCK, k_step, k_done, v_step, v_done)


def _attention(kern, name, key_weights, small_inputs, q, k, v, qs, knew, vnew, pool_k, pool_v, page_table,
               n_heads, with_gate):
    b, t, width = q.shape
    n_seq, n_pages = page_table.shape
    page_rows, hd = pool_k.shape[2], pool_k.shape[3]
    dec = qs.shape[1]
    past_len = n_pages * (page_rows // n_heads)
    assert b * n_heads == 2 * n_seq, "one prompt (batch, head) and one sample sweep per grid step"
    prompt = pl.BlockSpec((None, t, HEAD_DIM), lambda g, pt: (g // n_heads, 0, g % n_heads))
    per_seq = pl.BlockSpec((None, dec, width), lambda g, pt: (g // 2, 0, 0))
    small = lambda a: pl.BlockSpec(a.shape, lambda g, pt: (0,) * a.ndim)
    per_head = pl.BlockSpec((None, t, HEAD_DIM), lambda g, pt: (g % n_heads, 0, 0))
    hbm = pl.BlockSpec(memory_space=pl.ANY)
    scratch = [
        pltpu.VMEM((t, HEAD_DIM), BF16),
        pltpu.VMEM((HEAD_DIM, t), BF16),
        pltpu.VMEM((past_len + MOBA_BLOCK, LANES), F32),
        pltpu.VMEM((LANES, width), F32),
        pltpu.VMEM((8, LANES), F32),
        pltpu.VMEM((width, LANES), BF16),
        pltpu.VMEM((RING_SLOTS, PAGES_PER_STEP * page_rows, hd), pool_k.dtype),
        pltpu.SemaphoreType.DMA((RING_SLOTS,)),
        pltpu.VMEM((past_len // MOBA_BLOCK + 8, LANES), F32),
    ]
    if with_gate:
        scratch.append(pltpu.VMEM((past_len // MOBA_BLOCK, LANES), F32))
    return pl.pallas_call(
        kern,
        grid_spec=pltpu.PrefetchScalarGridSpec(
            num_scalar_prefetch=1,
            grid=(b * n_heads,),
            in_specs=[small(a) for a in small_inputs]
                     + [prompt, prompt, prompt, per_head, per_seq, per_seq, per_seq, hbm, hbm],
            out_specs=[prompt, per_seq],
            scratch_shapes=scratch,
        ),
        out_shape=[jax.ShapeDtypeStruct((b, t, width), BF16), jax.ShapeDtypeStruct((n_seq, dec, width), F32)],
        compiler_params=pltpu.CompilerParams(
            dimension_semantics=("arbitrary",), vmem_limit_bytes=VMEM_LIMIT),
        name=name,
    )(page_table, *small_inputs, q, k, v, _key_bias_columns(key_weights, t), qs, knew, vnew, pool_k, pool_v)


def _oproj_kernel(om_ref, od_ref, x_ref, w_ref, gpost_ref, gpre_ref, x1_ref, h2_ref):
    slab = x_ref.shape[0] // OPROJ_SLABS
    for s in range(OPROJ_SLABS):
        rows = slice(s * slab, (s + 1) * slab)
        o = jnp.concatenate([om_ref[rows, :].astype(BF16), od_ref[rows, :].astype(BF16)], axis=1)
        x1 = x_ref[rows, :] + _rms(_dot(o, w_ref[...]), gpost_ref[...])
        x1_ref[rows, :] = x1
        h2_ref[rows, :] = _rms(x1, gpre_ref[...]).astype(BF16)


def _oproj(om, od, x, w, gpost, gpre, tm):
    m, d = x.shape
    row = lambda width: pl.BlockSpec((tm, width), lambda i: (i, 0))
    const = lambda a: pl.BlockSpec(a.shape, lambda i: (0, 0))
    return pl.pallas_call(
        _oproj_kernel,
        grid=(m // tm,),
        in_specs=[row(om.shape[1]), row(od.shape[1]), row(d), const(w), const(gpost), const(gpre)],
        out_specs=[row(d), row(d)],
        out_shape=[jax.ShapeDtypeStruct((m, d), F32), jax.ShapeDtypeStruct((m, d), BF16)],
        compiler_params=pltpu.CompilerParams(
            dimension_semantics=("arbitrary",), vmem_limit_bytes=VMEM_LIMIT),
        name="oproj",
    )(om, od, x, w, gpost, gpre)


def _ffn_kernel(h_ref, x_ref, wg_ref, wu_ref, wd_ref, g_ref, o_ref, acc_ref):
    j = pl.program_id(1)

    @pl.when(j == 0)
    def _():
        acc_ref[...] = jnp.zeros(acc_ref.shape, F32)

    h = h_ref[...]
    gate = _dot(h, wg_ref[...])
    up = _dot(h, wu_ref[...])
    a = (gate * (1.0 / (1.0 + jnp.exp(-gate))) * up).astype(BF16)
    acc_ref[...] += _dot(a, wd_ref[...])

    @pl.when(j == pl.num_programs(1) - 1)
    def _():
        o_ref[...] = x_ref[...] + _rms(acc_ref[...], g_ref[...])


def _ffn(h, x, wg, wu, wd, g, tm, tf):
    m, d = x.shape
    f = wg.shape[1]
    return pl.pallas_call(
        _ffn_kernel,
        grid=(m // tm, f // tf),
        in_specs=[
            pl.BlockSpec((tm, d), lambda i, j: (i, 0)),
            pl.BlockSpec((tm, d), lambda i, j: (i, 0)),
            pl.BlockSpec((d, tf), lambda i, j: (0, j)),
            pl.BlockSpec((d, tf), lambda i, j: (0, j)),
            pl.BlockSpec((tf, d), lambda i, j: (j, 0)),
            pl.BlockSpec((1, d), lambda i, j: (0, 0)),
        ],
        out_specs=pl.BlockSpec((tm, d), lambda i, j: (i, 0)),
        out_shape=jax.ShapeDtypeStruct((m, d), F32),
        scratch_shapes=[pltpu.VMEM((tm, d), F32)],
        compiler_params=pltpu.CompilerParams(
            dimension_semantics=("arbitrary", "arbitrary"), vmem_limit_bytes=VMEM_LIMIT),
        name="ffn",
    )(h, x, wg, wu, wd, g)


def kernel(x_prompt, x_sample, cache_moba_k, cache_moba_v, cache_diff_k, cache_diff_v, page_table, g_pre_attn, w_in, lambda_q1, lambda_k1, lambda_q2, lambda_k2, g_subln, w_o, g_post_attn, g_pre_ffn, w_gate, w_up, w_down, g_post_ffn):
    batch, seq, d_model = x_prompt.shape
    dec_batch, dec_seq, _ = x_sample.shape
    depth, n_pool, page, h_moba, _ = cache_moba_k.shape
    h_diff = cache_diff_k.shape[3]
    n_heads = h_moba + h_diff
    wm, wd_ = h_moba * HEAD_DIM, h_diff * HEAD_DIM
    past_len = page_table.shape[1] * page
    assert seq % MOBA_BLOCK == 0 and past_len % MOBA_BLOCK == 0
    assert dec_seq <= MOBA_BLOCK and (page * PAGES_PER_STEP) % MOBA_BLOCK == 0
    assert wm == GROUP_WIDTH and wd_ == GROUP_WIDTH and h_moba * dec_seq <= LANES and 2 * h_diff * dec_seq <= LANES
    assert cache_diff_k.shape[4] == HEAD_DIM and cache_diff_v.shape[4] == HEAD_DIM

    slopes = _alibi_slopes(n_heads)
    s_moba, s_diff = slopes[0::2], slopes[1::2]
    pad_lanes = lambda v: jnp.pad(v, (0, LANES - v.shape[0])).reshape(1, LANES)
    lane_slope_m = pad_lanes(jnp.repeat(s_moba, dec_seq))
    lane_slope_d = pad_lanes(jnp.repeat(s_diff, 2 * dec_seq))
    pools = [c.reshape(depth, n_pool, -1, HEAD_DIM) for c in (cache_moba_k, cache_moba_v, cache_diff_k, cache_diff_v)]

    xp = x_prompt.reshape(batch * seq, d_model)
    xs = x_sample.reshape(dec_batch * dec_seq, d_model)
    tm_p, tm_s = 512, dec_batch * dec_seq
    kv_p = [[] for _ in range(4)]
    kv_s = [[] for _ in range(4)]
    for l in range(depth):
        lam_init = _lambda_init(l)
        row = lambda a: a[l].reshape(1, -1)
        lam_params = [row(lambda_q1), row(lambda_k1), row(lambda_q2), row(lambda_k2)]
        w_in_b, w_o_b = w_in[l].astype(BF16), w_o[l].astype(BF16)
        wg_b, wu_b, wd_b = w_gate[l].astype(BF16), w_up[l].astype(BF16), w_down[l].astype(BF16)

        proj_p = _proj(xp, row(g_pre_attn), w_in_b, 1024, 512)
        proj_s = _proj(xs, row(g_pre_attn), w_in_b, tm_s, GROUP_WIDTH)
        seq3 = lambda a: a.reshape(batch, seq, -1)
        dec3 = lambda a: a.reshape(dec_batch, dec_seq, -1)
        mq, mk, mv, dq, dk, dv = [seq3(a) for a in proj_p]
        mqs, mks, mvs, dqs, dks, dvs = [dec3(a) for a in proj_s]
        om_p, om_s = _attention(
            functools.partial(_moba_kernel, layer=l, n_heads=h_moba), "moba_attention", s_moba * (1.0 / MOBA_SCALE),
            [lane_slope_m], mq, mk, mv, mqs, mks, mvs, pools[0], pools[1], page_table, h_moba, True)
        od_p, od_s = _attention(
            functools.partial(_diff_kernel, layer=l, n_heads=h_diff, lam_init=lam_init), "diff_attention",
            s_diff * LOG2_E,
            [lane_slope_d] + lam_params + [row(g_subln)], dq, dk, dv, dqs, dks, dvs, pools[2], pools[3],
            page_table, h_diff, False)

        x1, h2 = _oproj(om_p.reshape(batch * seq, wm), od_p.reshape(batch * seq, wd_), xp, w_o_b,
                        row(g_post_attn), row(g_pre_ffn), tm_p)
        xp = _ffn(h2, x1, wg_b, wu_b, wd_b, row(g_post_ffn), tm_p, 512)
        x1, h2 = _oproj(om_s.reshape(tm_s, wm), od_s.reshape(tm_s, wd_), xs, w_o_b,
                        row(g_post_attn), row(g_pre_ffn), tm_s)
        xs = _ffn(h2, x1, wg_b, wu_b, wd_b, row(g_post_ffn), tm_s, 512)
        for dst, a in zip(kv_p, (mk, mv, dk, dv)):
            dst.append(a.reshape(batch, seq, -1, HEAD_DIM))
        for dst, a in zip(kv_s, (mks, mvs, dks, dvs)):
            dst.append(a.reshape(dec_batch, dec_seq, -1, HEAD_DIM))

    return (xp.reshape(batch, seq, d_model), xs.reshape(dec_batch, dec_seq, d_model),
            *[jnp.stack(t) for t in kv_p], *[jnp.stack(t) for t in kv_s])
```
